```python
import math
import jax, jax.numpy as jnp
from jax import lax
import numpy as np

D_MODEL = 4096
BATCH = 2
SEQ = 8192
DEPTH = 4
DEC_BATCH = 8
DEC_SEQ = 2048
PAST_LEN = 128

N_HEADS = 32
N_KV_HEADS = 8
HEAD_DIM = D_MODEL // N_HEADS
GROUP = N_HEADS // N_KV_HEADS
Q_DIM = N_HEADS * HEAD_DIM
KV_DIM = N_KV_HEADS * HEAD_DIM
QKV_DIM = Q_DIM + 2 * KV_DIM
WINDOW = 128
BLOCK = 128
ROPE_THETA = 10000.0
GRID_W = 64
N_EXPERTS = 16
D_EXPERT = D_MODEL // 2
CAPACITY_FACTOR = 2
N_MIXERS = 2
N_A = (DEPTH + 1) // 2
N_B = DEPTH // 2
EPS = 1e-6
NEG = -1e30

kernel_name = "hybrid_window_axial_ec_moe_encoder"


def rmsnorm(x, g):
    x32 = x.astype(jnp.float32)
    y = x32 * lax.rsqrt(jnp.mean(x32 * x32, axis=-1, keepdims=True) + EPS)
    return (y * g.astype(jnp.float32)).astype(x.dtype)


def rope_angles(pos, dim):
    inv_freq = ROPE_THETA ** (-jnp.arange(0, dim, 2, dtype=jnp.float32) / dim)
    return pos.astype(jnp.float32)[:, None] * inv_freq[None, :]


def apply_rope(x, ang):
    cos = jnp.cos(ang)[None, :, None, :].astype(x.dtype)
    sin = jnp.sin(ang)[None, :, None, :].astype(x.dtype)
    x1, x2 = jnp.split(x, 2, axis=-1)
    return jnp.concatenate([x1 * cos - x2 * sin, x2 * cos + x1 * sin], axis=-1)


def split_qkv(h, w_qkv):
    B, S, _ = h.shape
    qkv = h @ w_qkv
    q = qkv[..., :Q_DIM].reshape(B, S, N_HEADS, HEAD_DIM)
    k = qkv[..., Q_DIM:Q_DIM + KV_DIM].reshape(B, S, N_KV_HEADS, HEAD_DIM)
    v = qkv[..., Q_DIM + KV_DIM:].reshape(B, S, N_KV_HEADS, HEAD_DIM)
    return q, k, v


def window_sink_attention(h, w_qkv, w_o, sink):
    B, S, _ = h.shape
    nb = S // BLOCK
    q, k, v = split_qkv(h, w_qkv)
    ang = rope_angles(jnp.arange(S), HEAD_DIM)
    q = apply_rope(q, ang)
    k = apply_rope(k, ang)
    q = q.reshape(B, nb, BLOCK, N_KV_HEADS, GROUP, HEAD_DIM)

    def band(t):
        tp = jnp.pad(t, ((0, 0), (BLOCK, BLOCK), (0, 0), (0, 0)))
        tp = tp.reshape(B, nb + 2, BLOCK, N_KV_HEADS, HEAD_DIM)
        return jnp.concatenate([tp[:, :-2], tp[:, 1:-1], tp[:, 2:]], axis=2)

    kb, vb = band(k), band(v)
    scale = 1.0 / math.sqrt(HEAD_DIM)
    s = jnp.einsum('bnqkgd,bnjkd->bnkgqj', q, kb).astype(jnp.float32) * scale
    qpos = jnp.arange(nb)[:, None] * BLOCK + jnp.arange(BLOCK)[None, :]
    kpos = (jnp.arange(nb)[:, None] - 1) * BLOCK + jnp.arange(3 * BLOCK)[None, :]
    mask = (jnp.abs(kpos[:, None, :] - qpos[:, :, None]) <= WINDOW) \
        & (kpos[:, None, :] >= 0) & (kpos[:, None, :] < S)
    s = jnp.where(mask[None, :, None, None], s, NEG)
    sk = sink.astype(jnp.float32).reshape(1, 1, N_KV_HEADS, GROUP, 1, 1)
    m = jnp.maximum(jnp.max(s, axis=-1, keepdims=True), sk)
    p = jnp.exp(s - m)
    p = p / (jnp.sum(p, axis=-1, keepdims=True) + jnp.exp(sk - m))
    o = jnp.einsum('bnkgqj,bnjkd->bnqkgd', p.astype(vb.dtype), vb)
    return o.reshape(B, S, Q_DIM) @ w_o


def axial_qknorm_attention(h, w_qkv, w_o, q_gain, k_gain):
    B, S, _ = h.shape
    nb = S // BLOCK
    rows_n = S // GRID_W
    q, k, v = split_qkv(h, w_qkv)
    q = rmsnorm(q, q_gain)
    k = rmsnorm(k, k_gain)
    row = jnp.repeat(jnp.arange(rows_n), GRID_W)
    col = jnp.tile(jnp.arange(GRID_W), rows_n)
    half = HEAD_DIM // 2
    ang_r = rope_angles(row, half)
    ang_c = rope_angles(col, half)

    def axial(t):
        return jnp.concatenate([apply_rope(t[..., :half], ang_r),
                                apply_rope(t[..., half:], ang_c)], axis=-1)

    q, k = axial(q), axial(k)
    scale = 1.0 / math.sqrt(HEAD_DIM)
    qb = q.reshape(B, nb, BLOCK, N_KV_HEADS, GROUP, HEAD_DIM).transpose(1, 0, 2, 3, 4, 5)

    def one_block(qblk):
        s = jnp.einsum('bqkgd,bjkd->bkgqj', qblk, k).astype(jnp.float32) * scale
        p = jax.nn.softmax(s, axis=-1)
        return jnp.einsum('bkgqj,bjkd->bqkgd', p.astype(v.dtype), v)

    o = lax.map(one_block, qb)
    o = o.transpose(1, 0, 2, 3, 4, 5).reshape(B, S, Q_DIM)
    return o @ w_o


def expert_choice_moe(h, w_router, w_gate, w_up, w_down):
    B, S, D = h.shape
    n_tok = B * S
    cap = CAPACITY_FACTOR * n_tok // N_EXPERTS
    xf = h.reshape(n_tok, D)
    aff = jax.nn.softmax((xf @ w_router).astype(jnp.float32), axis=-1)
    gate, idx = lax.top_k(aff.T, cap)
    xe = xf[idx]
    hid = jax.nn.silu(jnp.einsum('ecd,edf->ecf', xe, w_gate)) * jnp.einsum('ecd,edf->ecf', xe, w_up)
    ye = jnp.einsum('ecf,efd->ecd', hid, w_down) * gate[..., None].astype(h.dtype)
    out = jnp.zeros((n_tok, D), h.dtype).at[idx.reshape(-1)].add(ye.reshape(-1, D))
    return out.reshape(B, S, D)


def trunk(x, norm_mix, norm_ffn, final_norm, a_w_qkv, a_w_o, a_sink,
          b_w_qkv, b_w_o, b_q_norm, b_k_norm, w_router, w_gate, w_up, w_down):
    for i in range(DEPTH):
        hn = rmsnorm(x, norm_mix[i])
        j = i // N_MIXERS
        if i % N_MIXERS == 0:
            x = x + window_sink_attention(hn, a_w_qkv[j], a_w_o[j], a_sink[j])
        else:
            x = x + axial_qknorm_attention(hn, b_w_qkv[j], b_w_o[j], b_q_norm[j], b_k_norm[j])
        hn = rmsnorm(x, norm_ffn[i])
        x = x + expert_choice_moe(hn, w_router[i], w_gate[i], w_up[i], w_down[i])
    return rmsnorm(x, final_norm)


def setup_inputs(seed: int = 0) -> dict:
    key = jax.random.key(seed)
    ks = jax.random.split(key, 16)
    f32 = jnp.float32
    nrm = lambda k, shape, s: jax.random.normal(k, shape, f32) * s
    return {
        "x_prompt": nrm(ks[0], (BATCH, SEQ, D_MODEL), 1.0),
        "x_sample": nrm(ks[1], (DEC_BATCH, DEC_SEQ, D_MODEL), 1.0),
        "norm_mix": 1.0 + nrm(ks[2], (DEPTH, D_MODEL), 0.01),
        "norm_ffn": 1.0 + nrm(ks[3], (DEPTH, D_MODEL), 0.01),
        "final_norm": 1.0 + nrm(ks[4], (D_MODEL,), 0.01),
        "a_w_qkv": nrm(ks[5], (N_A, D_MODEL, QKV_DIM), D_MODEL ** -0.5),
        "a_w_o": nrm(ks[6], (N_A, Q_DIM, D_MODEL), Q_DIM ** -0.5),
        "a_sink": nrm(ks[7], (N_A, N_HEADS), 0.5),
        "b_w_qkv": nrm(ks[8], (N_B, D_MODEL, QKV_DIM), D_MODEL ** -0.5),
        "b_w_o": nrm(ks[9], (N_B, Q_DIM, D_MODEL), Q_DIM ** -0.5),
        "b_q_norm": 1.0 + nrm(ks[10], (N_B, HEAD_DIM), 0.01),
        "b_k_norm": 1.0 + nrm(ks[11], (N_B, HEAD_DIM), 0.01),
        "w_router": nrm(ks[12], (DEPTH, D_MODEL, N_EXPERTS), D_MODEL ** -0.5),
        "w_gate": nrm(ks[13], (DEPTH, N_EXPERTS, D_MODEL, D_EXPERT), D_MODEL ** -0.5),
        "w_up": nrm(ks[14], (DEPTH, N_EXPERTS, D_MODEL, D_EXPERT), D_MODEL ** -0.5),
        "w_down": nrm(ks[15], (DEPTH, N_EXPERTS, D_EXPERT, D_MODEL), D_EXPERT ** -0.5),
    }


def reference(x_prompt, x_sample, norm_mix, norm_ffn, final_norm, a_w_qkv, a_w_o, a_sink,
              b_w_qkv, b_w_o, b_q_norm, b_k_norm, w_router, w_gate, w_up, w_down):
    y_prompt = trunk(x_prompt, norm_mix, norm_ffn, final_norm, a_w_qkv, a_w_o, a_sink,
                     b_w_qkv, b_w_o, b_q_norm, b_k_norm, w_router, w_gate, w_up, w_down)
    y_sample = trunk(x_sample, norm_mix, norm_ffn, final_norm, a_w_qkv, a_w_o, a_sink,
                     b_w_qkv, b_w_o, b_q_norm, b_k_norm, w_router, w_gate, w_up, w_down)
    return (y_prompt, y_sample)
```

```python
import functools
import math

import jax
import jax.numpy as jnp
from jax import lax
from jax.experimental import pallas as pl
from jax.experimental.pallas import tpu as pltpu

HEAD_DIM = 128
GROUP = 4
WINDOW = 128
ROPE_THETA = 10000.0
GRID_W = 64
N_EXPERTS = 16
CAPACITY_FACTOR = 2
N_MIXERS = 2
EPS = 1e-6
NEG = -1e30
LANES = 128
V7X_VMEM_LIMIT_BYTES = 56 * 1024 * 1024

BF16 = jnp.bfloat16
F32 = jnp.float32
_NT = (((1,), (1,)), ((), ()))


def _params(*sem):
    return pltpu.CompilerParams(dimension_semantics=sem, vmem_limit_bytes=V7X_VMEM_LIMIT_BYTES)


def _tile(n, target):
    t = min(n, target)
    while n % t:
        t -= 1
    return t


def _rms(x, g):
    return x * lax.rsqrt(jnp.mean(x * x, axis=-1, keepdims=True) + EPS) * g


def _qkv_kernel(x_ref, g_ref, w_ref, cos_ref, sin_ref, qg_ref, kg_ref, o_ref, hn_ref, *,
                n_q_tiles, n_k_tiles, qk_norm, axial):
    j = pl.program_id(1)

    @pl.when(j == 0)
    def _():
        hn_ref[...] = _rms(x_ref[...], g_ref[...]).astype(BF16)

    acc = jnp.dot(hn_ref[...], w_ref[...], preferred_element_type=F32)
    tm, tn = acc.shape

    def rotary(gain_ref):
        cos = cos_ref[...]
        sin = sin_ref[...]
        if axial:
            lane = lax.broadcasted_iota(jnp.int32, (tm, HEAD_DIM), 1)
            low_half = (lane % (HEAD_DIM // 2)) < (HEAD_DIM // 4)
        outs = []
        for h in range(tn // HEAD_DIM):
            a = acc[:, h * HEAD_DIM:(h + 1) * HEAD_DIM]
            if qk_norm:
                a = _rms(a, gain_ref[...])
            if axial:
                partner = jnp.where(low_half, pltpu.roll(a, 3 * HEAD_DIM // 4, 1),
                                    pltpu.roll(a, HEAD_DIM // 4, 1))
            else:
                partner = pltpu.roll(a, HEAD_DIM // 2, 1)
            outs.append(a * cos + partner * sin)
        return jnp.concatenate(outs, axis=1).astype(BF16)

    @pl.when(j < n_q_tiles)
    def _():
        o_ref[...] = rotary(qg_ref)

    @pl.when((j >= n_q_tiles) & (j < n_q_tiles + n_k_tiles))
    def _():
        o_ref[...] = rotary(kg_ref)

    @pl.when(j >= n_q_tiles + n_k_tiles)
    def _():
        o_ref[...] = acc.astype(BF16)


def _qkv(x2, g, w, cos, sin, q_gain, k_gain, *, seq_len, qk_norm, axial):
    n, d = x2.shape
    qkv_dim = w.shape[1]
    kv_dim = (qkv_dim - d) // 2
    tm = _tile(seq_len, 512)
    tn = _tile(kv_dim, 1024)
    s_tiles = seq_len // tm
    kern = functools.partial(_qkv_kernel, n_q_tiles=d // tn, n_k_tiles=kv_dim // tn,
                             qk_norm=qk_norm, axial=axial)
    return pl.pallas_call(
        kern,
        grid=(n // tm, qkv_dim // tn),
        in_specs=[
            pl.BlockSpec((tm, d), lambda i, j: (i, 0)),
            pl.BlockSpec((1, d), lambda i, j: (0, 0)),
            pl.BlockSpec((d, tn), lambda i, j: (0, j)),
            pl.BlockSpec((tm, HEAD_DIM), lambda i, j: (i % s_tiles, 0)),
            pl.BlockSpec((tm, HEAD_DIM), lambda i, j: (i % s_tiles, 0)),
            pl.BlockSpec((1, HEAD_DIM), lambda i, j: (0, 0)),
            pl.BlockSpec((1, HEAD_DIM), lambda i, j: (0, 0)),
        ],
        out_specs=pl.BlockSpec((tm, tn), lambda i, j: (i, j)),
        out_shape=jax.ShapeDtypeStruct((n, qkv_dim), BF16),
        scratch_shapes=[pltpu.VMEM((tm, d), BF16)],
        compiler_params=_params("parallel", "arbitrary"),
        name="qkv_proj",
    )(x2, g.reshape(1, d), w, cos, sin, q_gain.reshape(1, HEAD_DIM), k_gain.reshape(1, HEAD_DIM))


def _stack_heads(q):
    return jnp.concatenate([q[:, h * HEAD_DIM:(h + 1) * HEAD_DIM] for h in range(GROUP)], axis=0)


def _unstack_heads(o):
    t = o.shape[0] // GROUP
    return jnp.concatenate([o[h * t:(h + 1) * t] for h in range(GROUP)], axis=1)


def _attn_window_kernel(q_ref, kp_ref, kc_ref, kn_ref, vp_ref, vc_ref, vn_ref, sink_ref, o_ref, *, seq_len):
    i = pl.program_id(2)
    blk = WINDOW
    qs = _stack_heads(q_ref[0])
    k = jnp.concatenate([kp_ref[0], kc_ref[0], kn_ref[0]], axis=0)
    v = jnp.concatenate([vp_ref[0], vc_ref[0], vn_ref[0]], axis=0)
    s = lax.dot_general(qs, k, _NT, preferred_element_type=F32) * (1.0 / math.sqrt(HEAD_DIM))
    r = lax.broadcasted_iota(jnp.int32, s.shape, 0) % blk
    c = lax.broadcasted_iota(jnp.int32, s.shape, 1)
    s = jnp.where(jnp.abs(c - r - blk) <= WINDOW, s, NEG)
    s = jnp.where(c >= (1 - i) * blk, s, NEG)
    s = jnp.where(c < seq_len - (i - 1) * blk, s, NEG)
    sk = sink_ref[0]
    m = jnp.maximum(jnp.max(s, axis=-1, keepdims=True), sk)
    p = jnp.exp(s - m)
    denom = jnp.sum(p, axis=-1, keepdims=True) + jnp.exp(sk - m)
    o = jnp.dot(p.astype(BF16), v, preferred_element_type=F32) / denom
    o_ref[0] = _unstack_heads(o).astype(BF16)


def _attn_window(qkv, sink_rows, *, d_model):
    b, s, qkv_dim = qkv.shape
    n_kv = (qkv_dim - d_model) // (2 * HEAD_DIM)
    blk = WINDOW
    nb = s // blk
    kcol = d_model // HEAD_DIM
    vcol = kcol + n_kv
    qw = GROUP * HEAD_DIM

    def kv_spec(col0, shift):
        return pl.BlockSpec((1, blk, HEAD_DIM),
                            lambda bi, g, i: (bi, jnp.clip(i + shift, 0, nb - 1), col0 + g))

    return pl.pallas_call(
        functools.partial(_attn_window_kernel, seq_len=s),
        grid=(b, n_kv, nb),
        in_specs=[pl.BlockSpec((1, blk, qw), lambda bi, g, i: (bi, i, g)),
                  kv_spec(kcol, -1), kv_spec(kcol, 0), kv_spec(kcol, 1),
                  kv_spec(vcol, -1), kv_spec(vcol, 0), kv_spec(vcol, 1),
                  pl.BlockSpec((1, GROUP * blk, 1), lambda bi, g, i: (g, 0, 0))],
        out_specs=pl.BlockSpec((1, blk, qw), lambda bi, g, i: (bi, i, g)),
        out_shape=jax.ShapeDtypeStruct((b, s, d_model), BF16),
        compiler_params=_params("parallel", "parallel", "parallel"),
        name="attn_window",
    )(qkv, qkv, qkv, qkv, qkv, qkv, qkv, sink_rows)


def _attn_dense_kernel(q_ref, k_ref, v_ref, o_ref, m_ref, l_ref, acc_ref, *, tk):
    qs = _stack_heads(q_ref[0])
    c2 = math.log2(math.e) / math.sqrt(HEAD_DIM)
    m_ref[...] = jnp.full(m_ref.shape, NEG, F32)
    l_ref[...] = jnp.zeros(l_ref.shape, F32)
    acc_ref[...] = jnp.zeros(acc_ref.shape, F32)

    def body(c, carry):
        off = pl.multiple_of(c * tk, tk)
        k = k_ref[0, pl.ds(off, tk), :]
        v = v_ref[0, pl.ds(off, tk), :]
        s = lax.dot_general(qs, k, _NT, preferred_element_type=F32)
        m_prev = m_ref[...]
        m_new = jnp.maximum(m_prev, jnp.max(s, axis=-1, keepdims=True))
        alpha = jnp.exp2((m_prev - m_new) * c2)
        p = jnp.exp2((s - m_new) * c2)
        l_ref[...] = alpha * l_ref[...] + jnp.sum(p, axis=-1, keepdims=True)
        acc_ref[...] = alpha * acc_ref[...] + jnp.dot(p.astype(BF16), v, preferred_element_type=F32)
        m_ref[...] = m_new
        return carry

    lax.fori_loop(0, k_ref.shape[1] // tk, body, 0)
    o_ref[0] = _unstack_heads(acc_ref[...] / l_ref[...]).astype(BF16)


def _attn_dense(qkv, *, d_model):
    b, s, qkv_dim = qkv.shape
    n_kv = (qkv_dim - d_model) // (2 * HEAD_DIM)
    tq = _tile(s, 256)
    tk = _tile(s, 512)
    kcol = d_model // HEAD_DIM
    vcol = kcol + n_kv
    qw = GROUP * HEAD_DIM
    return pl.pallas_call(
        functools.partial(_attn_dense_kernel, tk=tk),
        grid=(b, n_kv, s // tq),
        in_specs=[pl.BlockSpec((1, tq, qw), lambda bi, g, i: (bi, i, g)),
                  pl.BlockSpec((1, s, HEAD_DIM), lambda bi, g, i: (bi, 0, kcol + g)),
                  pl.BlockSpec((1, s, HEAD_DIM), lambda bi, g, i: (bi, 0, vcol + g))],
        out_specs=pl.BlockSpec((1, tq, qw), lambda bi, g, i: (bi, i, g)),
        out_shape=jax.ShapeDtypeStruct((b, s, d_model), BF16),
        scratch_shapes=[pltpu.VMEM((GROUP * tq, 1), F32), pltpu.VMEM((GROUP * tq, 1), F32),
                        pltpu.VMEM((GROUP * tq, HEAD_DIM), F32)],
        compiler_params=_params("parallel", "parallel", "parallel"),
        name="attn_dense",
    )(qkv, qkv, qkv)


def _oproj_kernel(o_ref, w_ref, x_ref, out_ref):
    out_ref[...] = x_ref[...] + jnp.dot(o_ref[...], w_ref[...], preferred_element_type=F32)


def _oproj(o2, w, x2):
    n, q_dim = o2.shape
    d = w.shape[1]
    tm = _tile(n, 512)
    tn = _tile(d, 1024)
    return pl.pallas_call(
        _oproj_kernel,
        grid=(n // tm, d // tn),
        in_specs=[pl.BlockSpec((tm, q_dim), lambda i, j: (i, 0)),
                  pl.BlockSpec((q_dim, tn), lambda i, j: (0, j)),
                  pl.BlockSpec((tm, tn), lambda i, j: (i, j))],
        out_specs=pl.BlockSpec((tm, tn), lambda i, j: (i, j)),
        out_shape=jax.ShapeDtypeStruct((n, d), F32),
        compiler_params=_params("parallel", "parallel"),
        name="o_proj",
    )(o2, w, x2)


def _norm_router_kernel(x_ref, g_ref, wr_ref, hn_ref, aff_ref):
    hn = _rms(x_ref[...], g_ref[...])
    hn_ref[...] = hn
    logits = lax.dot_general(wr_ref[...], hn.astype(BF16), _NT, preferred_element_type=F32)
    e = jnp.exp(logits - jnp.max(logits, axis=0, keepdims=True))
    aff_ref[...] = e / jnp.sum(e, axis=0, keepdims=True)


def _norm_router(x2, g, wr_t):
    n, d = x2.shape
    tm = _tile(n, 256)
    return pl.pallas_call(
        _norm_router_kernel,
        grid=(n // tm,),
        in_specs=[pl.BlockSpec((tm, d), lambda i: (i, 0)),
                  pl.BlockSpec((1, d), lambda i: (0, 0)),
                  pl.BlockSpec((N_EXPERTS, d), lambda i: (0, 0))],
        out_specs=[pl.BlockSpec((tm, d), lambda i: (i, 0)),
                   pl.BlockSpec((N_EXPERTS, tm), lambda i: (0, i))],
        out_shape=[jax.ShapeDtypeStruct((n, d), F32), jax.ShapeDtypeStruct((N_EXPERTS, n), F32)],
        compiler_params=_params("parallel"),
        name="norm_router",
    )(x2, g.reshape(1, d), wr_t)


def _final_norm_kernel(x_ref, g_ref, o_ref):
    o_ref[...] = _rms(x_ref[...], g_ref[...])


def _final_norm(x2, g):
    n, d = x2.shape
    tm = _tile(n, 256)
    return pl.pallas_call(
        _final_norm_kernel,
        grid=(n // tm,),
        in_specs=[pl.BlockSpec((tm, d), lambda i: (i, 0)), pl.BlockSpec((1, d), lambda i: (0, 0))],
        out_specs=pl.BlockSpec((tm, d), lambda i: (i, 0)),
        out_shape=jax.ShapeDtypeStruct((n, d), F32),
        compiler_params=_params("parallel"),
        name="final_norm",
    )(x2, g.reshape(1, d))


def _select_kernel(aff_ref, idx_ref, gate_ref, *, cap):
    a = aff_ref[0]
    nb = a.shape[0]
    bits = lax.bitcast_convert_type(a, jnp.int32)

    def count(mask):
        return jnp.sum(jnp.sum(jnp.where(mask, 1.0, 0.0), axis=1, keepdims=True), axis=0, keepdims=True)

    thr = jnp.zeros((1, 1), jnp.int32)
    for bit in range(30, -1, -1):
        cand = thr | (1 << bit)
        thr = jnp.where(count(bits >= cand) >= cap, cand, thr)

    tri_u = jnp.where(lax.broadcasted_iota(jnp.int32, (LANES, LANES), 0)
                      <= lax.broadcasted_iota(jnp.int32, (LANES, LANES), 1), 1.0, 0.0).astype(BF16)
    rk = lax.broadcasted_iota(jnp.int32, (nb, nb), 0)
    ck = lax.broadcasted_iota(jnp.int32, (nb, nb), 1)
    tri_strict_l = jnp.where(ck < rk, 1.0, 0.0).astype(BF16)
    tri_u_nb = jnp.where(rk <= ck, 1.0, 0.0).astype(BF16)

    def local_cumsum(maskf):
        return jnp.dot(maskf.astype(BF16), tri_u, preferred_element_type=F32)

    gt = bits > thr
    eq = bits == thr
    eqf = jnp.where(eq, 1.0, 0.0)
    need = cap - count(gt)
    eq_local = local_cumsum(eqf)
    eq_tot = jnp.broadcast_to(eq_local[:, LANES - 1:LANES], (nb, LANES))
    eq_before = jnp.dot(tri_strict_l, eq_tot.astype(BF16), preferred_element_type=F32)
    eq_rank = eq_before + eq_local - eqf
    sel = gt | (eq & (eq_rank < need))
    self_ = jnp.where(sel, 1.0, 0.0)

    cl = local_cumsum(self_)
    ones8 = jnp.ones((8, LANES), BF16)
    cnt_row = lax.dot_general(ones8, self_.astype(BF16), _NT, preferred_element_type=F32)
    bend_row = jnp.dot(cnt_row.astype(BF16), tri_u_nb, preferred_element_type=F32)[0:1]
    bprev_row = bend_row - cnt_row[0:1]

    p = lax.broadcasted_iota(jnp.int32, (cap, nb), 0).astype(F32)
    hot_blk = jnp.where((bprev_row <= p) & (p < bend_row), 1.0, 0.0)
    kcol = lax.broadcasted_iota(jnp.int32, (cap, nb), 1).astype(F32)
    blk_id = jnp.sum(hot_blk * kcol, axis=1, keepdims=True)
    p_local = p[:, 0:1] - jnp.sum(hot_blk * bprev_row, axis=1, keepdims=True)
    hot_b = hot_blk.astype(BF16)
    row_cl = jnp.dot(hot_b, cl.astype(BF16), preferred_element_type=F32)
    row_sel = jnp.dot(hot_b, self_.astype(BF16), preferred_element_type=F32)
    hot_tok = jnp.where((row_sel > 0.5) & (row_cl == p_local + 1.0), 1.0, 0.0)
    jcol = lax.broadcasted_iota(jnp.int32, (cap, LANES), 1).astype(F32)
    tok = blk_id * float(LANES) + jnp.sum(hot_tok * jcol, axis=1, keepdims=True)
    a1 = a.astype(BF16)
    r1 = a - a1.astype(F32)
    a2 = r1.astype(BF16)
    a3 = (r1 - a2.astype(F32)).astype(BF16)
    row_a = (jnp.dot(hot_b, a1, preferred_element_type=F32) + jnp.dot(hot_b, a2, preferred_element_type=F32)
             + jnp.dot(hot_b, a3, preferred_element_type=F32))
    idx_ref[0] = tok.astype(jnp.int32)
    gate_ref[0] = jnp.sum(hot_tok * row_a, axis=1, keepdims=True)


def _select(aff_t):
    e, n = aff_t.shape
    cap = CAPACITY_FACTOR * n // N_EXPERTS
    nb = n // LANES
    idx, gate = pl.pallas_call(
        functools.partial(_select_kernel, cap=cap),
        grid=(e,),
        in_specs=[pl.BlockSpec((1, nb, LANES), lambda i: (i, 0, 0))],
        out_specs=[pl.BlockSpec((1, cap, 1), lambda i: (i, 0, 0)),
                   pl.BlockSpec((1, cap, 1), lambda i: (i, 0, 0))],
        out_shape=[jax.ShapeDtypeStruct((e, cap, 1), jnp.int32), jax.ShapeDtypeStruct((e, cap, 1), F32)],
        compiler_params=_params("parallel"),
        name="expert_select",
    )(aff_t.reshape(e, nb, LANES))
    return idx.reshape(e * cap), gate.reshape(e * cap, 1)


def _moe_kernel(idx_ref, hn_hbm, gate_ref, wg_ref, wu_ref, wd_ref, x_in_hbm, x_hbm,
                rows, xe_ref, acc_ref, hn_sem, xin_sem, xout_sem, *, tm, tiles_per_expert, n_tiles, dn):
    del x_in_hbm
    f = pl.program_id(2)
    n_f = pl.num_programs(2)
    t = pl.program_id(0) * tiles_per_expert + pl.program_id(1)
    slot = t % 2

    def start_rows(tile, make):
        def body(r, carry):
            make(idx_ref[tile * tm + r], r).start()
            return carry
        lax.fori_loop(0, tm, body, 0)

    def hn_copy(slot_):
        return lambda tok, r: pltpu.make_async_copy(
            hn_hbm.at[pl.ds(tok, 1)], rows.at[slot_, pl.ds(r, 1)], hn_sem.at[slot_])

    def x_in_copy(tok, r):
        return pltpu.make_async_copy(x_hbm.at[pl.ds(tok, 1)], rows.at[slot, pl.ds(r, 1)], xin_sem)

    def x_out_copy(tok, r):
        return pltpu.make_async_copy(rows.at[slot, pl.ds(r, 1)], x_hbm.at[pl.ds(tok, 1)], xout_sem)

    def wait_hn():
        pltpu.make_async_copy(hn_hbm.at[pl.ds(0, tm)], rows.at[slot], hn_sem.at[slot]).wait()

    def wait_x_in():
        pltpu.make_async_copy(x_hbm.at[pl.ds(0, tm)], rows.at[slot], xin_sem).wait()

    def wait_x_out():
        pltpu.make_async_copy(rows.at[slot], x_hbm.at[pl.ds(0, tm)], xout_sem).wait()

    @pl.when(f == 0)
    def _():
        @pl.when(t == 0)
        def _():
            start_rows(0, hn_copy(0))
        wait_hn()
        xe_ref[...] = rows[slot].astype(BF16)
        acc_ref[...] = jnp.zeros(acc_ref.shape, F32)

    @pl.when(f == jnp.minimum(1, n_f - 1))
    def _():
        @pl.when(t > 0)
        def _():
            wait_x_out()

        @pl.when(t + 1 < n_tiles)
        def _():
            start_rows(t + 1, hn_copy(1 - slot))

    @pl.when(f == n_f - 1)
    def _():
        start_rows(t, x_in_copy)

    xe = xe_ref[...]
    g = jnp.dot(xe, wg_ref[0], preferred_element_type=F32)
    u = jnp.dot(xe, wu_ref[0], preferred_element_type=F32)
    h = (g * jax.nn.sigmoid(g) * u).astype(BF16)
    for c in range(acc_ref.shape[1] // dn):
        cols = slice(c * dn, (c + 1) * dn)
        acc_ref[:, cols] += jnp.dot(h, wd_ref[0, :, cols], preferred_element_type=F32)

    @pl.when(f == n_f - 1)
    def _():
        wait_x_in()
        rows[slot] = rows[slot] + acc_ref[...] * gate_ref[...]
        start_rows(t, x_out_copy)

        @pl.when(t == n_tiles - 1)
        def _():
            wait_x_out()


def _moe(x2, hn, idx, gate, wg, wu, wd):
    n, d = x2.shape
    e, _, d_exp = wg.shape
    cap = idx.shape[0] // e
    tm = _tile(cap, 512)
    tf = _tile(d_exp, 256)
    tiles_per_expert = cap // tm
    kern = functools.partial(_moe_kernel, tm=tm, tiles_per_expert=tiles_per_expert,
                             n_tiles=e * tiles_per_expert, dn=_tile(d, 1024))
    grid_spec = pltpu.PrefetchScalarGridSpec(
        num_scalar_prefetch=1,
        grid=(e, tiles_per_expert, d_exp // tf),
        in_specs=[
            pl.BlockSpec(memory_space=pl.ANY),
            pl.BlockSpec((tm, 1), lambda ei, i, f, idx_: (ei * tiles_per_expert + i, 0)),
            pl.BlockSpec((1, d, tf), lambda ei, i, f, idx_: (ei, 0, f)),
            pl.BlockSpec((1, d, tf), lambda ei, i, f, idx_: (ei, 0, f)),
            pl.BlockSpec((1, tf, d), lambda ei, i, f, idx_: (ei, f, 0)),
            pl.BlockSpec(memory_space=pl.ANY),
        ],
        out_specs=pl.BlockSpec(memory_space=pl.ANY),
        scratch_shapes=[
            pltpu.VMEM((2, tm, d), F32),
            pltpu.VMEM((tm, d), BF16),
            pltpu.VMEM((tm, d), F32),
            pltpu.SemaphoreType.DMA((2,)),
            pltpu.SemaphoreType.DMA(()),
            pltpu.SemaphoreType.DMA(()),
        ],
    )
    return pl.pallas_call(
        kern,
        grid_spec=grid_spec,
        out_shape=jax.ShapeDtypeStruct((n, d), F32),
        input_output_aliases={6: 0},
        compiler_params=_params("arbitrary", "arbitrary", "arbitrary"),
        name="moe_experts",
    )(idx, hn, gate, wg, wu, wd, x2)


def _angles(pos, dim):
    inv_freq = ROPE_THETA ** (-jnp.arange(0, dim, 2, dtype=F32) / dim)
    return pos.astype(F32)[:, None] * inv_freq[None, :]


def _rope_tables_1d(s):
    ang = _angles(jnp.arange(s), HEAD_DIM)
    cos, sin = jnp.cos(ang), jnp.sin(ang)
    return jnp.concatenate([cos, cos], axis=1), jnp.concatenate([-sin, sin], axis=1)


def _rope_tables_axial(s):
    t = jnp.arange(s)
    half = HEAD_DIM // 2
    ar, ac = _angles(t // GRID_W, half), _angles(t % GRID_W, half)
    cos = jnp.concatenate([jnp.cos(ar), jnp.cos(ar), jnp.cos(ac), jnp.cos(ac)], axis=1)
    sin = jnp.concatenate([-jnp.sin(ar), jnp.sin(ar), -jnp.sin(ac), jnp.sin(ac)], axis=1)
    return cos, sin


def _trunk(x, norm_mix, norm_ffn, final_norm, a_w_qkv, a_w_o, a_sink, b_w_qkv, b_w_o, b_q_norm, b_k_norm,
           wr_t, w_gate, w_up, w_down):
    b, s, d = x.shape
    n = b * s
    depth = norm_mix.shape[0]
    qkv_dim = a_w_qkv.shape[2]
    cos_a, sin_a = _rope_tables_1d(s)
    cos_b, sin_b = _rope_tables_axial(s)
    ones = jnp.ones((HEAD_DIM,), F32)
    x2 = x.reshape(n, d)
    for i in range(depth):
        j = i // N_MIXERS
        if i % N_MIXERS == 0:
            qkv = _qkv(x2, norm_mix[i], a_w_qkv[j], cos_a, sin_a, ones, ones,
                       seq_len=s, qk_norm=False, axial=False)
            sink_rows = jnp.repeat(a_sink[j].astype(F32), WINDOW).reshape(-1, GROUP * WINDOW, 1)
            o = _attn_window(qkv.reshape(b, s, qkv_dim), sink_rows, d_model=d)
            x2 = _oproj(o.reshape(n, d), a_w_o[j], x2)
        else:
            qkv = _qkv(x2, norm_mix[i], b_w_qkv[j], cos_b, sin_b, b_q_norm[j], b_k_norm[j],
                       seq_len=s, qk_norm=True, axial=True)
            o = _attn_dense(qkv.reshape(b, s, qkv_dim), d_model=d)
            x2 = _oproj(o.reshape(n, d), b_w_o[j], x2)
        hn, aff_t = _norm_router(x2, norm_ffn[i], wr_t[i])
        idx, gate = _select(aff_t)
        x2 = _moe(x2, hn, idx, gate, w_gate[i], w_up[i], w_down[i])
    return _final_norm(x2, final_norm).reshape(b, s, d)


def kernel(x_prompt, x_sample, norm_mix, norm_ffn, final_norm, a_w_qkv, a_w_o, a_sink, b_w_qkv, b_w_o,
           b_q_norm, b_k_norm, w_router, w_gate, w_up, w_down):
    weights = (a_w_qkv.astype(BF16), a_w_o.astype(BF16), a_sink, b_w_qkv.astype(BF16), b_w_o.astype(BF16),
               b_q_norm, b_k_norm, jnp.swapaxes(w_router, 1, 2).astype(BF16),
               w_gate.astype(BF16), w_up.astype(BF16), w_down.astype(BF16))
    y_prompt = _trunk(x_prompt, norm_mix, norm_ffn, final_norm, *weights)
    y_sample = _trunk(x_sample, norm_mix, norm_ffn, final_norm, *weights)
    return (y_prompt, y_sample)
```

```python
import functools
import math

import jax
import jax.numpy as jnp
from jax import lax
from jax.experimental import pallas as pl
from jax.experimental.pallas import tpu as pltpu

HEAD_DIM = 128
GROUP = 4
WINDOW = 128
ROPE_THETA = 10000.0
GRID_W = 64
N_EXPERTS = 16
CAPACITY_FACTOR = 2
N_MIXERS = 2
EPS = 1e-6
NEG = -1e30
LANES = 128
SEQ_TILE = 512
V7X_VMEM_LIMIT_BYTES = 56 * 1024 * 1024

BF16 = jnp.bfloat16
F32 = jnp.float32
_NT = (((1,), (1,)), ((), ()))


def _params(*sem):
    return pltpu.CompilerParams(dimension_semantics=sem, vmem_limit_bytes=V7X_VMEM_LIMIT_BYTES)


def _tile(n, target):
    t = min(n, target)
    while n % t:
        t -= 1
    return t


def _rms(x, g):
    return x * lax.rsqrt(jnp.mean(x * x, axis=-1, keepdims=True) + EPS) * g


def _qkv_kernel(x_ref, g_ref, w_ref, cos_ref, sin_ref, qg_ref, kg_ref, o_ref, vt_ref, hn_ref, *,
                n_q_tiles, n_k_tiles, qk_norm, axial):
    j = pl.program_id(1)

    @pl.when(j == 0)
    def _():
        hn_ref[...] = _rms(x_ref[...], g_ref[...]).astype(BF16)

    acc = jnp.dot(hn_ref[...], w_ref[...], preferred_element_type=F32)
    tm, tn = acc.shape

    def rotary(gain_ref):
        cos = cos_ref[...]
        sin = sin_ref[...]
        if axial:
            lane = lax.broadcasted_iota(jnp.int32, (tm, HEAD_DIM), 1)
            low_half = (lane % (HEAD_DIM // 2)) < (HEAD_DIM // 4)
        outs = []
        for h in range(tn // HEAD_DIM):
            a = acc[:, h * HEAD_DIM:(h + 1) * HEAD_DIM]
            if qk_norm:
                a = _rms(a, gain_ref[...])
            if axial:
                partner = jnp.where(low_half, pltpu.roll(a, 3 * HEAD_DIM // 4, 1),
                                    pltpu.roll(a, HEAD_DIM // 4, 1))
            else:
                partner = pltpu.roll(a, HEAD_DIM // 2, 1)
            outs.append(a * cos + partner * sin)
        return jnp.concatenate(outs, axis=1).astype(BF16)

    @pl.when(j < n_q_tiles)
    def _():
        o_ref[...] = rotary(qg_ref)

    @pl.when((j >= n_q_tiles) & (j < n_q_tiles + n_k_tiles))
    def _():
        o_ref[...] = rotary(kg_ref)

    @pl.when(j >= n_q_tiles + n_k_tiles)
    def _():
        o_ref[...] = acc.astype(BF16)
        vt_ref[0] = acc.T.astype(BF16)


def _qkv(x2, g, w, cos, sin, q_gain, k_gain, *, seq_len, qk_norm, axial):
    n, d = x2.shape
    qkv_dim = w.shape[1]
    kv_dim = (qkv_dim - d) // 2
    tm = _tile(seq_len, SEQ_TILE)
    tn = _tile(kv_dim, 1024)
    s_tiles = seq_len // tm
    n_qk_tiles = (d + kv_dim) // tn
    kern = functools.partial(_qkv_kernel, n_q_tiles=d // tn, n_k_tiles=kv_dim // tn,
                             qk_norm=qk_norm, axial=axial)
    return pl.pallas_call(
        kern,
        grid=(n // tm, qkv_dim // tn),
        in_specs=[
            pl.BlockSpec((tm, d), lambda i, j: (i, 0)),
            pl.BlockSpec((1, d), lambda i, j: (0, 0)),
            pl.BlockSpec((d, tn), lambda i, j: (0, j)),
            pl.BlockSpec((tm, HEAD_DIM), lambda i, j: (i % s_tiles, 0)),
            pl.BlockSpec((tm, HEAD_DIM), lambda i, j: (i % s_tiles, 0)),
            pl.BlockSpec((1, HEAD_DIM), lambda i, j: (0, 0)),
            pl.BlockSpec((1, HEAD_DIM), lambda i, j: (0, 0)),
        ],
        out_specs=[pl.BlockSpec((tm, tn), lambda i, j: (i, j)),
                   pl.BlockSpec((1, tn, tm), lambda i, j: (i, jnp.maximum(j - n_qk_tiles, 0), 0))],
        out_shape=[jax.ShapeDtypeStruct((n, qkv_dim), BF16),
                   jax.ShapeDtypeStruct((n // tm, kv_dim, tm), BF16)],
        scratch_shapes=[pltpu.VMEM((tm, d), BF16)],
        compiler_params=_params("parallel", "arbitrary"),
        name="qkv_proj",
    )(x2, g.reshape(1, d), w, cos, sin, q_gain.reshape(1, HEAD_DIM), k_gain.reshape(1, HEAD_DIM))


def _heads_to_lanes(q):
    qf = q.astype(F32)
    return jnp.concatenate([qf[:, h * HEAD_DIM:(h + 1) * HEAD_DIM].T for h in range(GROUP)],
                           axis=1).astype(BF16)


def _lanes_to_heads(ot):
    t = ot.shape[1] // GROUP
    return jnp.concatenate([ot[:, h * t:(h + 1) * t].T for h in range(GROUP)], axis=1)


def _attn_window_kernel(q_ref, kp_ref, kc_ref, kn_ref, vp_ref, vc_ref, vn_ref, sink_ref, bias_ref, o_ref):
    log2e = math.log2(math.e)
    qw = GROUP * HEAD_DIM
    n_kv = q_ref.shape[2] // qw
    bias = bias_ref[0]

    def head_slice(ref, g):
        return ref[0, :, g * HEAD_DIM:(g + 1) * HEAD_DIM]

    def vt_slice(ref, g):
        return ref[0, 0, g * HEAD_DIM:(g + 1) * HEAD_DIM, :]

    sts = []
    for g in range(n_kv):
        qt = _heads_to_lanes(q_ref[0, :, g * qw:(g + 1) * qw])
        k = jnp.concatenate([head_slice(kp_ref, g), head_slice(kc_ref, g), head_slice(kn_ref, g)], axis=0)
        sts.append(jnp.dot(k, qt, preferred_element_type=F32))
    for g in range(n_kv):
        st = sts[g] * (log2e / math.sqrt(HEAD_DIM)) + bias
        sk = sink_ref[g] * log2e
        m = jnp.maximum(jnp.max(st, axis=0, keepdims=True), sk)
        p = jnp.exp2(st - m)
        denom = jnp.sum(p, axis=0, keepdims=True) + jnp.exp2(sk - m)
        vt = jnp.concatenate([vt_slice(vp_ref, g), vt_slice(vc_ref, g), vt_slice(vn_ref, g)], axis=1)
        ot = jnp.dot(vt, p.astype(BF16), preferred_element_type=F32) / denom
        o_ref[0, :, g * qw:(g + 1) * qw] = _lanes_to_heads(ot).astype(BF16)


def _window_bias():
    blk = WINDOW
    c = jnp.arange(3 * blk)[:, None]
    r = (jnp.arange(GROUP * blk) % blk)[None, :]
    band = jnp.abs(c - r - blk) <= WINDOW
    variants = []
    for last in (False, True):
        for first in (False, True):
            ok = band
            if first:
                ok = ok & (c >= blk)
            if last:
                ok = ok & (c < 2 * blk)
            variants.append(jnp.where(ok, 0.0, NEG).astype(F32))
    return jnp.stack([variants[0], variants[1], variants[2], variants[3]])


def _attn_window(qkv, vt, sink_rows, *, d_model):
    b, s, qkv_dim = qkv.shape
    n_kv = (qkv_dim - d_model) // (2 * HEAD_DIM)
    blk = WINDOW
    nb = s // blk
    sub = vt.shape[3] // blk
    kv_dim = n_kv * HEAD_DIM
    kcol = d_model // kv_dim
    bias = _window_bias()

    def k_spec(shift):
        return pl.BlockSpec((1, blk, kv_dim), lambda bi, i: (bi, jnp.clip(i + shift, 0, nb - 1), kcol))

    def vt_spec(shift):
        def index(bi, i):
            kb = jnp.clip(i + shift, 0, nb - 1)
            return (bi, kb // sub, 0, kb % sub)
        return pl.BlockSpec((1, 1, kv_dim, blk), index)

    def bias_index(bi, i):
        return ((i == 0).astype(jnp.int32) + 2 * (i == nb - 1).astype(jnp.int32), 0, 0)

    return pl.pallas_call(
        _attn_window_kernel,
        grid=(b, nb),
        in_specs=[pl.BlockSpec((1, blk, d_model), lambda bi, i: (bi, i, 0)),
                  k_spec(-1), k_spec(0), k_spec(1), vt_spec(-1), vt_spec(0), vt_spec(1),
                  pl.BlockSpec((n_kv, 1, GROUP * blk), lambda bi, i: (0, 0, 0)),
                  pl.BlockSpec((1, 3 * blk, GROUP * blk), bias_index)],
        out_specs=pl.BlockSpec((1, blk, d_model), lambda bi, i: (bi, i, 0)),
        out_shape=jax.ShapeDtypeStruct((b, s, d_model), BF16),
        compiler_params=_params("parallel", "parallel"),
        name="attn_window",
    )(qkv, qkv, qkv, qkv, vt, vt, vt, sink_rows, bias)


def _attn_dense_kernel(q_ref, k_ref, vt_ref, o_ref, qt_ref, acc0_ref, acc1_ref, acc2_ref, acc3_ref, *, unroll):
    tk = vt_ref.shape[3]
    c2 = math.log2(math.e) / math.sqrt(HEAD_DIM)
    qt_ref[...] = _heads_to_lanes(q_ref[0])
    for acc_ref in (acc0_ref, acc1_ref, acc2_ref, acc3_ref):
        acc_ref[...] = jnp.zeros(acc_ref.shape, F32)

    tq = qt_ref.shape[1] // GROUP
    accs = (acc0_ref, acc1_ref, acc2_ref, acc3_ref)

    def scores(c):
        off = pl.multiple_of(c * tk, tk)
        k = k_ref[0, pl.ds(off, tk), :]
        return [jnp.dot(k, qt_ref[:, h * tq:(h + 1) * tq], preferred_element_type=F32)
                for h in range(GROUP)]

    def update(c, h, st, m_prev, l_prev):
        m_new = jnp.maximum(m_prev, jnp.max(st, axis=0, keepdims=True))
        alpha = jnp.exp2((m_prev - m_new) * c2)
        p = jnp.exp2((st - m_new) * c2)
        accs[h][...] = alpha * accs[h][...] + jnp.dot(vt_ref[0, c], p.astype(BF16),
                                                      preferred_element_type=F32)
        return m_new, alpha * l_prev + jnp.sum(p, axis=0, keepdims=True)

    def body(cc, carry):
        ms, ls = list(carry[0]), list(carry[1])
        c0 = cc * unroll
        st = scores(c0)
        for u in range(unroll):
            st_next = [None] * GROUP
            for h in range(GROUP):
                ms[h], ls[h] = update(c0 + u, h, st[h], ms[h], ls[h])
                if u + 1 < unroll and h == 0:
                    st_next = scores(c0 + u + 1)
            st = st_next
        return tuple(ms), tuple(ls)

    init = (tuple(jnp.full((1, tq), NEG, F32) for _ in range(GROUP)),
            tuple(jnp.zeros((1, tq), F32) for _ in range(GROUP)))
    _, ls = lax.fori_loop(0, vt_ref.shape[1] // unroll, body, init)
    ot = jnp.concatenate([accs[h][...] / ls[h] for h in range(GROUP)], axis=1)
    o_ref[0] = _lanes_to_heads(ot).astype(BF16)


def _attn_dense(qkv, vt, *, d_model):
    b, s, qkv_dim = qkv.shape
    n_kv = (qkv_dim - d_model) // (2 * HEAD_DIM)
    tq = _tile(s, 256)
    n_kt, tk = vt.shape[1], vt.shape[3]
    kcol = d_model // HEAD_DIM
    qw = GROUP * HEAD_DIM
    return pl.pallas_call(
        functools.partial(_attn_dense_kernel, unroll=_tile(n_kt, 4)),
        grid=(b, n_kv, s // tq),
        in_specs=[pl.BlockSpec((1, tq, qw), lambda bi, g, i: (bi, i, g)),
                  pl.BlockSpec((1, s, HEAD_DIM), lambda bi, g, i: (bi, 0, kcol + g)),
                  pl.BlockSpec((1, n_kt, HEAD_DIM, tk), lambda bi, g, i: (bi, 0, g, 0))],
        out_specs=pl.BlockSpec((1, tq, qw), lambda bi, g, i: (bi, i, g)),
        out_shape=jax.ShapeDtypeStruct((b, s, d_model), BF16),
        scratch_shapes=[pltpu.VMEM((HEAD_DIM, GROUP * tq), BF16)]
        + [pltpu.VMEM((HEAD_DIM, tq), F32) for _ in range(GROUP)],
        compiler_params=_params("parallel", "parallel", "parallel"),
        name="attn_dense",
    )(qkv, qkv, vt)


def _oproj_kernel(o_ref, w_ref, x_ref, out_ref):
    out_ref[...] = x_ref[...] + jnp.dot(o_ref[...], w_ref[...], preferred_element_type=F32)


def _oproj(o2, w, x2):
    n, q_dim = o2.shape
    d = w.shape[1]
    tm = _tile(n, 512)
    tn = _tile(d, 1024)
    return pl.pallas_call(
        _oproj_kernel,
        grid=(n // tm, d // tn),
        in_specs=[pl.BlockSpec((tm, q_dim), lambda i, j: (i, 0)),
                  pl.BlockSpec((q_dim, tn), lambda i, j: (0, j)),
                  pl.BlockSpec((tm, tn), lambda i, j: (i, j))],
        out_specs=pl.BlockSpec((tm, tn), lambda i, j: (i, j)),
        out_shape=jax.ShapeDtypeStruct((n, d), F32),
        compiler_params=_params("parallel", "parallel"),
        name="o_proj",
    )(o2, w, x2)


def _norm_router_kernel(x_ref, g_ref, wr_ref, hn_ref, aff_ref):
    hn = _rms(x_ref[...], g_ref[...])
    hn_ref[...] = hn
    logits = lax.dot_general(wr_ref[...], hn.astype(BF16), _NT, preferred_element_type=F32)
    e = jnp.exp(logits - jnp.max(logits, axis=0, keepdims=True))
    aff_ref[...] = e / jnp.sum(e, axis=0, keepdims=True)


def _norm_router(x2, g, wr_t):
    n, d = x2.shape
    tm = _tile(n, 256)
    return pl.pallas_call(
        _norm_router_kernel,
        grid=(n // tm,),
        in_specs=[pl.BlockSpec((tm, d), lambda i: (i, 0)),
                  pl.BlockSpec((1, d), lambda i: (0, 0)),
                  pl.BlockSpec((N_EXPERTS, d), lambda i: (0, 0))],
        out_specs=[pl.BlockSpec((tm, d), lambda i: (i, 0)),
                   pl.BlockSpec((N_EXPERTS, tm), lambda i: (0, i))],
        out_shape=[jax.ShapeDtypeStruct((n, d), F32), jax.ShapeDtypeStruct((N_EXPERTS, n), F32)],
        compiler_params=_params("parallel"),
        name="norm_router",
    )(x2, g.reshape(1, d), wr_t)


def _final_norm_kernel(x_ref, g_ref, o_ref):
    o_ref[...] = _rms(x_ref[...], g_ref[...])


def _final_norm(x2, g):
    n, d = x2.shape
    tm = _tile(n, 256)
    return pl.pallas_call(
        _final_norm_kernel,
        grid=(n // tm,),
        in_specs=[pl.BlockSpec((tm, d), lambda i: (i, 0)), pl.BlockSpec((1, d), lambda i: (0, 0))],
        out_specs=pl.BlockSpec((tm, d), lambda i: (i, 0)),
        out_shape=jax.ShapeDtypeStruct((n, d), F32),
        compiler_params=_params("parallel"),
        name="final_norm",
    )(x2, g.reshape(1, d))


def _select_kernel(aff_ref, idx_ref, gate_ref, *, cap):
    a = aff_ref[0]
    nb = a.shape[0]
    bits = lax.bitcast_convert_type(a, jnp.int32)

    def count(mask):
        return jnp.sum(jnp.sum(jnp.where(mask, 1.0, 0.0), axis=1, keepdims=True), axis=0, keepdims=True)

    thr = jnp.zeros((1, 1), jnp.int32)
    for bit in range(30, -1, -1):
        cand = thr | (1 << bit)
        thr = jnp.where(count(bits >= cand) >= cap, cand, thr)

    tri_u = jnp.where(lax.broadcasted_iota(jnp.int32, (LANES, LANES), 0)
                      <= lax.broadcasted_iota(jnp.int32, (LANES, LANES), 1), 1.0, 0.0).astype(BF16)
    rk = lax.broadcasted_iota(jnp.int32, (nb, nb), 0)
    ck = lax.broadcasted_iota(jnp.int32, (nb, nb), 1)
    tri_strict_l = jnp.where(ck < rk, 1.0, 0.0).astype(BF16)
    tri_u_nb = jnp.where(rk <= ck, 1.0, 0.0).astype(BF16)

    def local_cumsum(maskf):
        return jnp.dot(maskf.astype(BF16), tri_u, preferred_element_type=F32)

    gt = bits > thr
    eq = bits == thr
    eqf = jnp.where(eq, 1.0, 0.0)
    need = cap - count(gt)
    eq_local = local_cumsum(eqf)
    eq_tot = jnp.broadcast_to(eq_local[:, LANES - 1:LANES], (nb, LANES))
    eq_before = jnp.dot(tri_strict_l, eq_tot.astype(BF16), preferred_element_type=F32)
    eq_rank = eq_before + eq_local - eqf
    sel = gt | (eq & (eq_rank < need))
    self_ = jnp.where(sel, 1.0, 0.0)

    cl = local_cumsum(self_)
    ones8 = jnp.ones((8, LANES), BF16)
    cnt_row = lax.dot_general(ones8, self_.astype(BF16), _NT, preferred_element_type=F32)
    bend_row = jnp.dot(cnt_row.astype(BF16), tri_u_nb, preferred_element_type=F32)[0:1]
    bprev_row = bend_row - cnt_row[0:1]

    p = lax.broadcasted_iota(jnp.int32, (cap, nb), 0).astype(F32)
    hot_blk = jnp.where((bprev_row <= p) & (p < bend_row), 1.0, 0.0)
    kcol = lax.broadcasted_iota(jnp.int32, (cap, nb), 1).astype(F32)
    blk_id = jnp.sum(hot_blk * kcol, axis=1, keepdims=True)
    p_local = p[:, 0:1] - jnp.sum(hot_blk * bprev_row, axis=1, keepdims=True)
    hot_b = hot_blk.astype(BF16)
    row_cl = jnp.dot(hot_b, cl.astype(BF16), preferred_element_type=F32)
    row_sel = jnp.dot(hot_b, self_.astype(BF16), preferred_element_type=F32)
    hot_tok = jnp.where((row_sel > 0.5) & (row_cl == p_local + 1.0), 1.0, 0.0)
    jcol = lax.broadcasted_iota(jnp.int32, (cap, LANES), 1).astype(F32)
    tok = blk_id * float(LANES) + jnp.sum(hot_tok * jcol, axis=1, keepdims=True)
    a1 = a.astype(BF16)
    r1 = a - a1.astype(F32)
    a2 = r1.astype(BF16)
    a3 = (r1 - a2.astype(F32)).astype(BF16)
    row_a = (jnp.dot(hot_b, a1, preferred_element_type=F32) + jnp.dot(hot_b, a2, preferred_element_type=F32)
             + jnp.dot(hot_b, a3, preferred_element_type=F32))
    idx_ref[0] = tok.astype(jnp.int32)
    gate_ref[0] = jnp.sum(hot_tok * row_a, axis=1, keepdims=True)


def _select(aff_t):
    e, n = aff_t.shape
    cap = CAPACITY_FACTOR * n // N_EXPERTS
    nb = n // LANES
    idx, gate = pl.pallas_call(
        functools.partial(_select_kernel, cap=cap),
        grid=(e,),
        in_specs=[pl.BlockSpec((1, nb, LANES), lambda i: (i, 0, 0))],
        out_specs=[pl.BlockSpec((1, cap, 1), lambda i: (i, 0, 0)),
                   pl.BlockSpec((1, cap, 1), lambda i: (i, 0, 0))],
        out_shape=[jax.ShapeDtypeStruct((e, cap, 1), jnp.int32), jax.ShapeDtypeStruct((e, cap, 1), F32)],
        compiler_params=_params("parallel"),
        name="expert_select",
    )(aff_t.reshape(e, nb, LANES))
    return idx.reshape(e * cap), gate.reshape(e * cap, 1)


def _moe_kernel(idx_ref, hn_hbm, gate_ref, wg_ref, wu_ref, wd_ref, x_in_hbm, x_hbm,
                rows, xe_ref, acc_ref, hn_sem, xin_sem, xout_sem, *, tm, tiles_per_expert, n_tiles, dn):
    del x_in_hbm
    f = pl.program_id(2)
    n_f = pl.num_programs(2)
    t = pl.program_id(0) * tiles_per_expert + pl.program_id(1)
    slot = t % 2

    def start_rows(tile, make):
        def body(r, carry):
            make(idx_ref[tile * tm + r], r).start()
            return carry
        lax.fori_loop(0, tm, body, 0, unroll=8)

    def hn_copy(slot_):
        return lambda tok, r: pltpu.make_async_copy(
            hn_hbm.at[pl.ds(tok, 1)], rows.at[slot_, pl.ds(r, 1)], hn_sem.at[slot_])

    def x_in_copy(tok, r):
        return pltpu.make_async_copy(x_hbm.at[pl.ds(tok, 1)], rows.at[slot, pl.ds(r, 1)], xin_sem)

    def x_out_copy(tok, r):
        return pltpu.make_async_copy(rows.at[slot, pl.ds(r, 1)], x_hbm.at[pl.ds(tok, 1)], xout_sem)

    def wait_hn():
        pltpu.make_async_copy(hn_hbm.at[pl.ds(0, tm)], rows.at[slot], hn_sem.at[slot]).wait()

    def wait_x_in():
        pltpu.make_async_copy(x_hbm.at[pl.ds(0, tm)], rows.at[slot], xin_sem).wait()

    def wait_x_out():
        pltpu.make_async_copy(rows.at[slot], x_hbm.at[pl.ds(0, tm)], xout_sem).wait()

    @pl.when(f == 0)
    def _():
        @pl.when(t == 0)
        def _():
            start_rows(0, hn_copy(0))
        wait_hn()
        xe_ref[...] = rows[slot].astype(BF16)
        acc_ref[...] = jnp.zeros(acc_ref.shape, F32)

    @pl.when(f == jnp.minimum(1, n_f - 1))
    def _():
        @pl.when(t > 0)
        def _():
            wait_x_out()

        @pl.when(t + 1 < n_tiles)
        def _():
            start_rows(t + 1, hn_copy(1 - slot))

    @pl.when(f == n_f - 1)
    def _():
        start_rows(t, x_in_copy)

    xe = xe_ref[...]
    g = jnp.dot(xe, wg_ref[0], preferred_element_type=F32)
    u = jnp.dot(xe, wu_ref[0], preferred_element_type=F32)
    h = (g * jax.nn.sigmoid(g) * u).astype(BF16)
    for c in range(acc_ref.shape[1] // dn):
        cols = slice(c * dn, (c + 1) * dn)
        acc_ref[:, cols] += jnp.dot(h, wd_ref[0, :, cols], preferred_element_type=F32)

    @pl.when(f == n_f - 1)
    def _():
        wait_x_in()
        rows[slot] = rows[slot] + acc_ref[...] * gate_ref[...]
        start_rows(t, x_out_copy)

        @pl.when(t == n_tiles - 1)
        def _():
            wait_x_out()


def _moe(x2, hn, idx, gate, wg, wu, wd, layer):
    n, d = x2.shape
    e = N_EXPERTS
    d_exp = wg.shape[2]
    e0 = layer * e
    cap = idx.shape[0] // e
    tm = _tile(cap, 512)
    tf = _tile(d_exp, 256)
    tiles_per_expert = cap // tm
    kern = functools.partial(_moe_kernel, tm=tm, tiles_per_expert=tiles_per_expert,
                             n_tiles=e * tiles_per_expert, dn=_tile(d, 1024))
    grid_spec = pltpu.PrefetchScalarGridSpec(
        num_scalar_prefetch=1,
        grid=(e, tiles_per_expert, d_exp // tf),
        in_specs=[
            pl.BlockSpec(memory_space=pl.ANY),
            pl.BlockSpec((tm, 1), lambda ei, i, f, idx_: (ei * tiles_per_expert + i, 0)),
            pl.BlockSpec((1, d, tf), lambda ei, i, f, idx_: (e0 + ei, 0, f)),
            pl.BlockSpec((1, d, tf), lambda ei, i, f, idx_: (e0 + ei, 0, f)),
            pl.BlockSpec((1, tf, d), lambda ei, i, f, idx_: (e0 + ei, f, 0)),
            pl.BlockSpec(memory_space=pl.ANY),
        ],
        out_specs=pl.BlockSpec(memory_space=pl.ANY),
        scratch_shapes=[
            pltpu.VMEM((2, tm, d), F32),
            pltpu.VMEM((tm, d), BF16),
            pltpu.VMEM((tm, d), F32),
            pltpu.SemaphoreType.DMA((2,)),
            pltpu.SemaphoreType.DMA(()),
            pltpu.SemaphoreType.DMA(()),
        ],
    )
    return pl.pallas_call(
        kern,
        grid_spec=grid_spec,
        out_shape=jax.ShapeDtypeStruct((n, d), F32),
        input_output_aliases={6: 0},
        compiler_params=_params("arbitrary", "arbitrary", "arbitrary"),
        name="moe_experts",
    )(idx, hn, gate, wg, wu, wd, x2)


def _angles(pos, dim):
    inv_freq = ROPE_THETA ** (-jnp.arange(0, dim, 2, dtype=F32) / dim)
    return pos.astype(F32)[:, None] * inv_freq[None, :]


def _rope_tables_1d(s):
    ang = _angles(jnp.arange(s), HEAD_DIM)
    cos, sin = jnp.cos(ang), jnp.sin(ang)
    return jnp.concatenate([cos, cos], axis=1), jnp.concatenate([-sin, sin], axis=1)


def _rope_tables_axial(s):
    t = jnp.arange(s)
    half = HEAD_DIM // 2
    ar, ac = _angles(t // GRID_W, half), _angles(t % GRID_W, half)
    cos = jnp.concatenate([jnp.cos(ar), jnp.cos(ar), jnp.cos(ac), jnp.cos(ac)], axis=1)
    sin = jnp.concatenate([-jnp.sin(ar), jnp.sin(ar), -jnp.sin(ac), jnp.sin(ac)], axis=1)
    return cos, sin


def _trunk(x, norm_mix, norm_ffn, final_norm, a_w_qkv, a_w_o, a_sink, b_w_qkv, b_w_o, b_q_norm, b_k_norm,
           wr_t, w_gate, w_up, w_down):
    b, s, d = x.shape
    n = b * s
    depth = norm_mix.shape[0]
    qkv_dim = a_w_qkv.shape[2]
    cos_a, sin_a = _rope_tables_1d(s)
    cos_b, sin_b = _rope_tables_axial(s)
    ones = jnp.ones((HEAD_DIM,), F32)
    x2 = x.reshape(n, d)
    for i in range(depth):
        j = i // N_MIXERS
        if i % N_MIXERS == 0:
            qkv, vt = _qkv(x2, norm_mix[i], a_w_qkv[j], cos_a, sin_a, ones, ones,
                           seq_len=s, qk_norm=False, axial=False)
            vt = vt.reshape(b, -1, vt.shape[1], vt.shape[2])
            sink_rows = jnp.repeat(a_sink[j].astype(F32), WINDOW).reshape(-1, 1, GROUP * WINDOW)
            o = _attn_window(qkv.reshape(b, s, qkv_dim), vt, sink_rows, d_model=d)
            x2 = _oproj(o.reshape(n, d), a_w_o[j], x2)
        else:
            qkv, vt = _qkv(x2, norm_mix[i], b_w_qkv[j], cos_b, sin_b, b_q_norm[j], b_k_norm[j],
                           seq_len=s, qk_norm=True, axial=True)
            vt = vt.reshape(b, -1, vt.shape[1], vt.shape[2])
            o = _attn_dense(qkv.reshape(b, s, qkv_dim), vt, d_model=d)
            x2 = _oproj(o.reshape(n, d), b_w_o[j], x2)
        hn, aff_t = _norm_router(x2, norm_ffn[i], wr_t[i])
        idx, gate = _select(aff_t)
        x2 = _moe(x2, hn, idx, gate, w_gate, w_up, w_down, i)
    return _final_norm(x2, final_norm).reshape(b, s, d)


def kernel(x_prompt, x_sample, norm_mix, norm_ffn, final_norm, a_w_qkv, a_w_o, a_sink, b_w_qkv, b_w_o,
           b_q_norm, b_k_norm, w_router, w_gate, w_up, w_down):
    def experts(w):
        return w.astype(BF16).reshape((-1,) + w.shape[2:])

    weights = (a_w_qkv.astype(BF16), a_w_o.astype(BF16), a_sink, b_w_qkv.astype(BF16), b_w_o.astype(BF16),
               b_q_norm, b_k_norm, jnp.swapaxes(w_router, 1, 2).astype(BF16),
               experts(w_gate), experts(w_up), experts(w_down))
    y_prompt = _trunk(x_prompt, norm_mix, norm_ffn, final_norm, *weights)
    y_sample = _trunk(x_sample, norm_mix, norm_ffn, final_norm, *weights)
    return (y_prompt, y_sample)
```

```python
import functools
import math

import jax
import jax.numpy as jnp
from jax import lax
from jax.experimental import pallas as pl
from jax.experimental.pallas import tpu as pltpu

HEAD_DIM = 128
GROUP = 4
WINDOW = 128
ROPE_THETA = 10000.0
GRID_W = 64
N_EXPERTS = 16
CAPACITY_FACTOR = 2
N_MIXERS = 2
EPS = 1e-6
NEG = -1e30
LANES = 128
SEQ_TILE = 512
MOE_STEPS = 8
ONES_ROWS = 16
DENSE_Q_SCALE = math.log2(math.e) / math.sqrt(HEAD_DIM)
V7X_VMEM_LIMIT_BYTES = 56 * 1024 * 1024

BF16 = jnp.bfloat16
F32 = jnp.float32
_NT = (((1,), (1,)), ((), ()))


def _params(*sem):
    return pltpu.CompilerParams(dimension_semantics=sem, vmem_limit_bytes=V7X_VMEM_LIMIT_BYTES)


def _tile(n, target):
    t = min(n, target)
    while n % t:
        t -= 1
    return t


def _rms(x, g):
    return x * lax.rsqrt(jnp.mean(x * x, axis=-1, keepdims=True) + EPS) * g


def _qkv_kernel(x_ref, g_ref, w_ref, cos_ref, sin_ref, qg_ref, kg_ref, o_ref, vt_ref, hn_ref, *,
                n_q_tiles, n_k_tiles, qk_norm, axial, q_scale):
    j = pl.program_id(1)

    @pl.when(j == 0)
    def _():
        hn_ref[...] = _rms(x_ref[...], g_ref[...]).astype(BF16)

    acc = jnp.dot(hn_ref[...], w_ref[...], preferred_element_type=F32)
    tm, tn = acc.shape

    def rotary(gain_ref, scale=1.0):
        cos = cos_ref[...]
        sin = sin_ref[...]
        if axial:
            lane = lax.broadcasted_iota(jnp.int32, (tm, HEAD_DIM), 1)
            low_half = (lane % (HEAD_DIM // 2)) < (HEAD_DIM // 4)
        outs = []
        for h in range(tn // HEAD_DIM):
            a = acc[:, h * HEAD_DIM:(h + 1) * HEAD_DIM]
            if qk_norm:
                a = _rms(a, gain_ref[...])
            if axial:
                partner = jnp.where(low_half, pltpu.roll(a, 3 * HEAD_DIM // 4, 1),
                                    pltpu.roll(a, HEAD_DIM // 4, 1))
            else:
                partner = pltpu.roll(a, HEAD_DIM // 2, 1)
            rot = a * cos + partner * sin
            outs.append(rot if scale == 1.0 else rot * scale)
        return jnp.concatenate(outs, axis=1).astype(BF16)

    @pl.when(j < n_q_tiles)
    def _():
        o_ref[...] = rotary(qg_ref, q_scale)

    @pl.when((j >= n_q_tiles) & (j < n_q_tiles + n_k_tiles))
    def _():
        o_ref[...] = rotary(kg_ref)

    @pl.when(j >= n_q_tiles + n_k_tiles)
    def _():
        o_ref[...] = acc.astype(BF16)
        vt_ref[0] = acc.T.astype(BF16)


def _qkv(x2, g, w, cos, sin, q_gain, k_gain, *, seq_len, qk_norm, axial, q_scale=1.0):
    n, d = x2.shape
    qkv_dim = w.shape[1]
    kv_dim = (qkv_dim - d) // 2
    tm = _tile(seq_len, SEQ_TILE)
    tn = _tile(kv_dim, 1024)
    s_tiles = seq_len // tm
    n_qk_tiles = (d + kv_dim) // tn
    kern = functools.partial(_qkv_kernel, n_q_tiles=d // tn, n_k_tiles=kv_dim // tn,
                             qk_norm=qk_norm, axial=axial, q_scale=q_scale)
    return pl.pallas_call(
        kern,
        grid=(n // tm, qkv_dim // tn),
        in_specs=[
            pl.BlockSpec((tm, d), lambda i, j: (i, 0)),
            pl.BlockSpec((1, d), lambda i, j: (0, 0)),
            pl.BlockSpec((d, tn), lambda i, j: (0, j)),
            pl.BlockSpec((tm, HEAD_DIM), lambda i, j: (i % s_tiles, 0)),
            pl.BlockSpec((tm, HEAD_DIM), lambda i, j: (i % s_tiles, 0)),
            pl.BlockSpec((1, HEAD_DIM), lambda i, j: (0, 0)),
            pl.BlockSpec((1, HEAD_DIM), lambda i, j: (0, 0)),
        ],
        out_specs=[pl.BlockSpec((tm, tn), lambda i, j: (i, j)),
                   pl.BlockSpec((1, tn, tm), lambda i, j: (i, jnp.maximum(j - n_qk_tiles, 0), 0))],
        out_shape=[jax.ShapeDtypeStruct((n, qkv_dim), BF16),
                   jax.ShapeDtypeStruct((n // tm, kv_dim, tm), BF16)],
        scratch_shapes=[pltpu.VMEM((tm, d), BF16)],
        compiler_params=_params("parallel", "arbitrary"),
        name="qkv_proj",
    )(x2, g.reshape(1, d), w, cos, sin, q_gain.reshape(1, HEAD_DIM), k_gain.reshape(1, HEAD_DIM))


def _heads_to_lanes(q):
    qf = q.astype(F32)
    return jnp.concatenate([qf[:, h * HEAD_DIM:(h + 1) * HEAD_DIM].T for h in range(GROUP)],
                           axis=1).astype(BF16)


def _lanes_to_heads(ot):
    t = ot.shape[1] // GROUP
    return jnp.concatenate([ot[:, h * t:(h + 1) * t].T for h in range(GROUP)], axis=1)


def _attn_window_kernel(q_ref, kp_ref, kc_ref, kn_ref, vp_ref, vc_ref, vn_ref, sink_ref, bias_ref, o_ref):
    log2e = math.log2(math.e)
    qw = GROUP * HEAD_DIM
    n_kv = q_ref.shape[2] // qw
    bias = bias_ref[0]

    def head_slice(ref, g):
        return ref[0, :, g * HEAD_DIM:(g + 1) * HEAD_DIM]

    def vt_slice(ref, g):
        return ref[0, 0, g * HEAD_DIM:(g + 1) * HEAD_DIM, :]

    sts = []
    for g in range(n_kv):
        qt = _heads_to_lanes(q_ref[0, :, g * qw:(g + 1) * qw])
        k = jnp.concatenate([head_slice(kp_ref, g), head_slice(kc_ref, g), head_slice(kn_ref, g)], axis=0)
        sts.append(jnp.dot(k, qt, preferred_element_type=F32))
    for g in range(n_kv):
        st = sts[g] * (log2e / math.sqrt(HEAD_DIM)) + bias
        sk = sink_ref[g] * log2e
        m = jnp.maximum(jnp.max(st, axis=0, keepdims=True), sk)
        p = jnp.exp2(st - m)
        denom = jnp.sum(p, axis=0, keepdims=True) + jnp.exp2(sk - m)
        vt = jnp.concatenate([vt_slice(vp_ref, g), vt_slice(vc_ref, g), vt_slice(vn_ref, g)], axis=1)
        ot = jnp.dot(vt, p.astype(BF16), preferred_element_type=F32) / denom
        o_ref[0, :, g * qw:(g + 1) * qw] = _lanes_to_heads(ot).astype(BF16)


def _window_bias():
    blk = WINDOW
    c = jnp.arange(3 * blk)[:, None]
    r = (jnp.arange(GROUP * blk) % blk)[None, :]
    band = jnp.abs(c - r - blk) <= WINDOW
    variants = []
    for last in (False, True):
        for first in (False, True):
            ok = band
            if first:
                ok = ok & (c >= blk)
            if last:
                ok = ok & (c < 2 * blk)
            variants.append(jnp.where(ok, 0.0, NEG).astype(F32))
    return jnp.stack([variants[0], variants[1], variants[2], variants[3]])


def _attn_window(qkv, vt, sink_rows, *, d_model):
    b, s, qkv_dim = qkv.shape
    n_kv = (qkv_dim - d_model) // (2 * HEAD_DIM)
    blk = WINDOW
    nb = s // blk
    sub = vt.shape[3] // blk
    kv_dim = n_kv * HEAD_DIM
    kcol = d_model // kv_dim
    bias = _window_bias()

    def k_spec(shift):
        return pl.BlockSpec((1, blk, kv_dim), lambda bi, i: (bi, jnp.clip(i + shift, 0, nb - 1), kcol))

    def vt_spec(shift):
        def index(bi, i):
            kb = jnp.clip(i + shift, 0, nb - 1)
            return (bi, kb // sub, 0, kb % sub)
        return pl.BlockSpec((1, 1, kv_dim, blk), index)

    def bias_index(bi, i):
        return ((i == 0).astype(jnp.int32) + 2 * (i == nb - 1).astype(jnp.int32), 0, 0)

    return pl.pallas_call(
        _attn_window_kernel,
        grid=(b, nb),
        in_specs=[pl.BlockSpec((1, blk, d_model), lambda bi, i: (bi, i, 0)),
                  k_spec(-1), k_spec(0), k_spec(1), vt_spec(-1), vt_spec(0), vt_spec(1),
                  pl.BlockSpec((n_kv, 1, GROUP * blk), lambda bi, i: (0, 0, 0)),
                  pl.BlockSpec((1, 3 * blk, GROUP * blk), bias_index)],
        out_specs=pl.BlockSpec((1, blk, d_model), lambda bi, i: (bi, i, 0)),
        out_shape=jax.ShapeDtypeStruct((b, s, d_model), BF16),
        compiler_params=_params("parallel", "parallel"),
        name="attn_window",
    )(qkv, qkv, qkv, qkv, vt, vt, vt, sink_rows, bias)


def _attn_dense_kernel(q_ref, k_ref, vt_ref, o_ref, qt_ref, acc0_ref, acc1_ref, acc2_ref, acc3_ref, *, unroll):
    tk = vt_ref.shape[3]
    qt_ref[...] = _heads_to_lanes(q_ref[0])
    for acc_ref in (acc0_ref, acc1_ref, acc2_ref, acc3_ref):
        acc_ref[...] = jnp.zeros(acc_ref.shape, F32)

    tq = qt_ref.shape[1] // GROUP
    accs = (acc0_ref, acc1_ref, acc2_ref, acc3_ref)
    ones = jnp.ones((ONES_ROWS, tk), BF16)

    def scores(c):
        off = pl.multiple_of(c * tk, tk)
        k = k_ref[0, pl.ds(off, tk), :]
        return [jnp.dot(k, qt_ref[:, h * tq:(h + 1) * tq], preferred_element_type=F32)
                for h in range(GROUP)]

    def update(c, h, st, m_prev):
        m_new = jnp.maximum(m_prev, jnp.max(st, axis=0, keepdims=True))
        p = jnp.exp2(st - m_new).astype(BF16)
        vt1 = jnp.concatenate([vt_ref[0, c], ones], axis=0)
        accs[h][...] = jnp.exp2(m_prev - m_new) * accs[h][...] + jnp.dot(vt1, p, preferred_element_type=F32)
        return m_new

    def body(cc, ms):
        ms = list(ms)
        c0 = cc * unroll
        st = scores(c0)
        for u in range(unroll):
            st_next = [None] * GROUP
            for h in range(GROUP):
                ms[h] = update(c0 + u, h, st[h], ms[h])
                if u + 1 < unroll and h == 0:
                    st_next = scores(c0 + u + 1)
            st = st_next
        return tuple(ms)

    init = tuple(jnp.full((1, tq), NEG, F32) for _ in range(GROUP))
    lax.fori_loop(0, vt_ref.shape[1] // unroll, body, init)
    ot = jnp.concatenate([accs[h][0:HEAD_DIM] / accs[h][HEAD_DIM:HEAD_DIM + 1] for h in range(GROUP)], axis=1)
    o_ref[0] = _lanes_to_heads(ot).astype(BF16)


def _attn_dense(qkv, vt, *, d_model):
    b, s, qkv_dim = qkv.shape
    n_kv = (qkv_dim - d_model) // (2 * HEAD_DIM)
    tq = _tile(s, 256)
    n_kt, tk = vt.shape[1], vt.shape[3]
    kcol = d_model // HEAD_DIM
    qw = GROUP * HEAD_DIM
    return pl.pallas_call(
        functools.partial(_attn_dense_kernel, unroll=_tile(n_kt, 4)),
        grid=(b, n_kv, s // tq),
        in_specs=[pl.BlockSpec((1, tq, qw), lambda bi, g, i: (bi, i, g)),
                  pl.BlockSpec((1, s, HEAD_DIM), lambda bi, g, i: (bi, 0, kcol + g)),
                  pl.BlockSpec((1, n_kt, HEAD_DIM, tk), lambda bi, g, i: (bi, 0, g, 0))],
        out_specs=pl.BlockSpec((1, tq, qw), lambda bi, g, i: (bi, i, g)),
        out_shape=jax.ShapeDtypeStruct((b, s, d_model), BF16),
        scratch_shapes=[pltpu.VMEM((HEAD_DIM, GROUP * tq), BF16)]
        + [pltpu.VMEM((HEAD_DIM + ONES_ROWS, tq), F32) for _ in range(GROUP)],
        compiler_params=_params("parallel", "parallel", "parallel"),
        name="attn_dense",
    )(qkv, qkv, vt)


def _oproj_kernel(o_ref, w_ref, x_ref, out_ref):
    out_ref[...] = x_ref[...] + jnp.dot(o_ref[...], w_ref[...], preferred_element_type=F32)


def _oproj(o2, w, x2):
    n, q_dim = o2.shape
    d = w.shape[1]
    tm = _tile(n, 512)
    tn = _tile(d, 1024)
    return pl.pallas_call(
        _oproj_kernel,
        grid=(n // tm, d // tn),
        in_specs=[pl.BlockSpec((tm, q_dim), lambda i, j: (i, 0)),
                  pl.BlockSpec((q_dim, tn), lambda i, j: (0, j)),
                  pl.BlockSpec((tm, tn), lambda i, j: (i, j))],
        out_specs=pl.BlockSpec((tm, tn), lambda i, j: (i, j)),
        out_shape=jax.ShapeDtypeStruct((n, d), F32),
        compiler_params=_params("parallel", "parallel"),
        name="o_proj",
    )(o2, w, x2)


def _norm_router_kernel(x_ref, g_ref, wr_ref, hn_ref, aff_ref):
    hn = _rms(x_ref[...], g_ref[...])
    hn_ref[...] = hn
    logits = lax.dot_general(wr_ref[...], hn.astype(BF16), _NT, preferred_element_type=F32)
    e = jnp.exp(logits - jnp.max(logits, axis=0, keepdims=True))
    aff_ref[...] = e / jnp.sum(e, axis=0, keepdims=True)


def _norm_router(x2, g, wr_t):
    n, d = x2.shape
    tm = _tile(n, 256)
    return pl.pallas_call(
        _norm_router_kernel,
        grid=(n // tm,),
        in_specs=[pl.BlockSpec((tm, d), lambda i: (i, 0)),
                  pl.BlockSpec((1, d), lambda i: (0, 0)),
                  pl.BlockSpec((N_EXPERTS, d), lambda i: (0, 0))],
        out_specs=[pl.BlockSpec((tm, d), lambda i: (i, 0)),
                   pl.BlockSpec((N_EXPERTS, tm), lambda i: (0, i))],
        out_shape=[jax.ShapeDtypeStruct((n, d), F32), jax.ShapeDtypeStruct((N_EXPERTS, n), F32)],
        compiler_params=_params("parallel"),
        name="norm_router",
    )(x2, g.reshape(1, d), wr_t)


def _final_norm_kernel(x_ref, g_ref, o_ref):
    o_ref[...] = _rms(x_ref[...], g_ref[...])


def _final_norm(x2, g):
    n, d = x2.shape
    tm = _tile(n, 256)
    return pl.pallas_call(
        _final_norm_kernel,
        grid=(n // tm,),
        in_specs=[pl.BlockSpec((tm, d), lambda i: (i, 0)), pl.BlockSpec((1, d), lambda i: (0, 0))],
        out_specs=pl.BlockSpec((tm, d), lambda i: (i, 0)),
        out_shape=jax.ShapeDtypeStruct((n, d), F32),
        compiler_params=_params("parallel"),
        name="final_norm",
    )(x2, g.reshape(1, d))


def _select_kernel(aff_ref, idx_ref, gate_ref, *, cap):
    a = aff_ref[0]
    nb = a.shape[0]
    bits = lax.bitcast_convert_type(a, jnp.int32)

    def count(mask):
        return jnp.sum(jnp.sum(jnp.where(mask, 1.0, 0.0), axis=1, keepdims=True), axis=0, keepdims=True)

    thr = jnp.zeros((1, 1), jnp.int32)
    for bit in range(30, -1, -1):
        cand = thr | (1 << bit)
        thr = jnp.where(count(bits >= cand) >= cap, cand, thr)

    tri_u = jnp.where(lax.broadcasted_iota(jnp.int32, (LANES, LANES), 0)
                      <= lax.broadcasted_iota(jnp.int32, (LANES, LANES), 1), 1.0, 0.0).astype(BF16)
    rk = lax.broadcasted_iota(jnp.int32, (nb, nb), 0)
    ck = lax.broadcasted_iota(jnp.int32, (nb, nb), 1)
    tri_strict_l = jnp.where(ck < rk, 1.0, 0.0).astype(BF16)
    tri_u_nb = jnp.where(rk <= ck, 1.0, 0.0).astype(BF16)

    def local_cumsum(maskf):
        return jnp.dot(maskf.astype(BF16), tri_u, preferred_element_type=F32)

    gt = bits > thr
    eq = bits == thr
    eqf = jnp.where(eq, 1.0, 0.0)
    need = cap - count(gt)
    eq_local = local_cumsum(eqf)
    eq_tot = jnp.broadcast_to(eq_local[:, LANES - 1:LANES], (nb, LANES))
    eq_before = jnp.dot(tri_strict_l, eq_tot.astype(BF16), preferred_element_type=F32)
    eq_rank = eq_before + eq_local - eqf
    sel = gt | (eq & (eq_rank < need))
    self_ = jnp.where(sel, 1.0, 0.0)

    cl = local_cumsum(self_)
    ones8 = jnp.ones((8, LANES), BF16)
    cnt_row = lax.dot_general(ones8, self_.astype(BF16), _NT, preferred_element_type=F32)
    bend_row = jnp.dot(cnt_row.astype(BF16), tri_u_nb, preferred_element_type=F32)[0:1]
    bprev_row = bend_row - cnt_row[0:1]

    p = lax.broadcasted_iota(jnp.int32, (cap, nb), 0).astype(F32)
    hot_blk = jnp.where((bprev_row <= p) & (p < bend_row), 1.0, 0.0)
    kcol = lax.broadcasted_iota(jnp.int32, (cap, nb), 1).astype(F32)
    blk_id = jnp.sum(hot_blk * kcol, axis=1, keepdims=True)
    p_local = p[:, 0:1] - jnp.sum(hot_blk * bprev_row, axis=1, keepdims=True)
    hot_b = hot_blk.astype(BF16)
    row_cl = jnp.dot(hot_b, cl.astype(BF16), preferred_element_type=F32)
    row_sel = jnp.dot(hot_b, self_.astype(BF16), preferred_element_type=F32)
    hot_tok = jnp.where((row_sel > 0.5) & (row_cl == p_local + 1.0), 1.0, 0.0)
    jcol = lax.broadcasted_iota(jnp.int32, (cap, LANES), 1).astype(F32)
    tok = blk_id * float(LANES) + jnp.sum(hot_tok * jcol, axis=1, keepdims=True)
    a1 = a.astype(BF16)
    r1 = a - a1.astype(F32)
    a2 = r1.astype(BF16)
    a3 = (r1 - a2.astype(F32)).astype(BF16)
    row_a = (jnp.dot(hot_b, a1, preferred_element_type=F32) + jnp.dot(hot_b, a2, preferred_element_type=F32)
             + jnp.dot(hot_b, a3, preferred_element_type=F32))
    idx_ref[0] = tok.astype(jnp.int32)
    gate_ref[0] = jnp.sum(hot_tok * row_a, axis=1, keepdims=True)


def _select(aff_t):
    e, n = aff_t.shape
    cap = CAPACITY_FACTOR * n // N_EXPERTS
    nb = n // LANES
    idx, gate = pl.pallas_call(
        functools.partial(_select_kernel, cap=cap),
        grid=(e,),
        in_specs=[pl.BlockSpec((1, nb, LANES), lambda i: (i, 0, 0))],
        out_specs=[pl.BlockSpec((1, cap, 1), lambda i: (i, 0, 0)),
                   pl.BlockSpec((1, cap, 1), lambda i: (i, 0, 0))],
        out_shape=[jax.ShapeDtypeStruct((e, cap, 1), jnp.int32), jax.ShapeDtypeStruct((e, cap, 1), F32)],
        compiler_params=_params("parallel"),
        name="expert_select",
    )(aff_t.reshape(e, nb, LANES))
    return idx.reshape(e * cap), gate.reshape(e * cap, 1)


def _moe_kernel(idx_ref, hn_hbm, gate_ref, wg_ref, wu_ref, wd_ref, x_in_hbm, x_hbm,
                rows, xe_ref, acc_ref, hn_sem, xin_sem, xout_sem, *, tm, tiles_per_expert, n_tiles, dn):
    del x_in_hbm
    f = pl.program_id(2)
    t = pl.program_id(0) * tiles_per_expert + pl.program_id(1)
    slot = t % 2
    other = 1 - slot
    half = tm // 2

    def hn_copy(tile, slot_):
        return lambda row: pltpu.make_async_copy(
            hn_hbm.at[pl.ds(idx_ref[tile * tm + row], 1)], rows.at[slot_, pl.ds(row, 1)], hn_sem.at[slot_])

    def x_in_copy(row):
        return pltpu.make_async_copy(
            x_hbm.at[pl.ds(idx_ref[t * tm + row], 1)], rows.at[slot, pl.ds(row, 1)], xin_sem)

    def x_out_copy(tile, slot_):
        return lambda row: pltpu.make_async_copy(
            rows.at[slot_, pl.ds(row, 1)], x_hbm.at[pl.ds(idx_ref[tile * tm + row], 1)], xout_sem)

    def start_all(make):
        def body(r, carry):
            make(r).start()
            return carry
        lax.fori_loop(0, tm, body, 0, unroll=8)

    def start_half(make, part):
        for r in range(half):
            make(part * half + r).start()

    def wait_hn():
        pltpu.make_async_copy(hn_hbm.at[pl.ds(0, tm)], rows.at[slot], hn_sem.at[slot]).wait()

    def wait_x_in():
        pltpu.make_async_copy(x_hbm.at[pl.ds(0, tm)], rows.at[slot], xin_sem).wait()

    def wait_x_out():
        pltpu.make_async_copy(rows.at[slot], x_hbm.at[pl.ds(0, tm)], xout_sem).wait()

    @pl.when(f == 0)
    def _():
        @pl.when(t == 0)
        def _():
            start_all(hn_copy(0, 0))
        wait_hn()
        xe_ref[...] = rows[slot].astype(BF16)
        acc_ref[...] = jnp.zeros(acc_ref.shape, F32)

    @pl.when((f == 3) & (t > 0))
    def _():
        wait_x_out()

    def step(issue=None):
        if issue is not None:
            issue()
        xe = xe_ref[...]
        g = jnp.dot(xe, wg_ref[0], preferred_element_type=F32)
        u = jnp.dot(xe, wu_ref[0], preferred_element_type=F32)
        h = (g * jax.nn.sigmoid(g) * u).astype(BF16)
        for c in range(acc_ref.shape[1] // dn):
            cols = slice(c * dn, (c + 1) * dn)
            acc_ref[:, cols] += jnp.dot(h, wd_ref[0, :, cols], preferred_element_type=F32)

    issue_out = (f <= 1) & (t > 0)
    issue_x_in = (f == 3) | (f == 4)
    issue_hn = ((f == 5) | (f == 6)) & (t + 1 < n_tiles)

    @pl.when(issue_out)
    def _():
        step(lambda: start_half(x_out_copy(t - 1, other), f))

    @pl.when(issue_x_in)
    def _():
        step(lambda: start_half(x_in_copy, f - 3))

    @pl.when(issue_hn)
    def _():
        step(lambda: start_half(hn_copy(t + 1, other), f - 5))

    @pl.when(jnp.logical_not(issue_out | issue_x_in | issue_hn))
    def _():
        step()

    @pl.when(f == MOE_STEPS - 1)
    def _():
        wait_x_in()
        rows[slot] = rows[slot] + acc_ref[...] * gate_ref[...]

        @pl.when(t == n_tiles - 1)
        def _():
            start_all(x_out_copy(t, slot))
            wait_x_out()


def _moe(x2, hn, idx, gate, wg, wu, wd, layer):
    n, d = x2.shape
    e = N_EXPERTS
    d_exp = wg.shape[2]
    e0 = layer * e
    cap = idx.shape[0] // e
    tm = _tile(cap, 512)
    assert d_exp % MOE_STEPS == 0 and tm % 2 == 0
    tf = d_exp // MOE_STEPS
    tiles_per_expert = cap // tm
    kern = functools.partial(_moe_kernel, tm=tm, tiles_per_expert=tiles_per_expert,
                             n_tiles=e * tiles_per_expert, dn=_tile(d, 1024))
    grid_spec = pltpu.PrefetchScalarGridSpec(
        num_scalar_prefetch=1,
        grid=(e, tiles_per_expert, d_exp // tf),
        in_specs=[
            pl.BlockSpec(memory_space=pl.ANY),
            pl.BlockSpec((tm, 1), lambda ei, i, f, idx_: (ei * tiles_per_expert + i, 0)),
            pl.BlockSpec((1, d, tf), lambda ei, i, f, idx_: (e0 + ei, 0, f)),
            pl.BlockSpec((1, d, tf), lambda ei, i, f, idx_: (e0 + ei, 0, f)),
            pl.BlockSpec((1, tf, d), lambda ei, i, f, idx_: (e0 + ei, f, 0)),
            pl.BlockSpec(memory_space=pl.ANY),
        ],
        out_specs=pl.BlockSpec(memory_space=pl.ANY),
        scratch_shapes=[
            pltpu.VMEM((2, tm, d), F32),
            pltpu.VMEM((tm, d), BF16),
            pltpu.VMEM((tm, d), F32),
            pltpu.SemaphoreType.DMA((2,)),
            pltpu.SemaphoreType.DMA(()),
            pltpu.SemaphoreType.DMA(()),
        ],
    )
    return pl.pallas_call(
        kern,
        grid_spec=grid_spec,
        out_shape=jax.ShapeDtypeStruct((n, d), F32),
        input_output_aliases={6: 0},
        compiler_params=_params("arbitrary", "arbitrary", "arbitrary"),
        name="moe_experts",
    )(idx, hn, gate, wg, wu, wd, x2)


def _angles(pos, dim):
    inv_freq = ROPE_THETA ** (-jnp.arange(0, dim, 2, dtype=F32) / dim)
    return pos.astype(F32)[:, None] * inv_freq[None, :]


def _rope_tables_1d(s):
    ang = _angles(jnp.arange(s), HEAD_DIM)
    cos, sin = jnp.cos(ang), jnp.sin(ang)
    return jnp.concatenate([cos, cos], axis=1), jnp.concatenate([-sin, sin], axis=1)


def _rope_tables_axial(s):
    t = jnp.arange(s)
    half = HEAD_DIM // 2
    ar, ac = _angles(t // GRID_W, half), _angles(t % GRID_W, half)
    cos = jnp.concatenate([jnp.cos(ar), jnp.cos(ar), jnp.cos(ac), jnp.cos(ac)], axis=1)
    sin = jnp.concatenate([-jnp.sin(ar), jnp.sin(ar), -jnp.sin(ac), jnp.sin(ac)], axis=1)
    return cos, sin


def _trunk(x, norm_mix, norm_ffn, final_norm, a_w_qkv, a_w_o, a_sink, b_w_qkv, b_w_o, b_q_norm, b_k_norm,
           wr_t, w_gate, w_up, w_down):
    b, s, d = x.shape
    n = b * s
    depth = norm_mix.shape[0]
    qkv_dim = a_w_qkv.shape[2]
    cos_a, sin_a = _rope_tables_1d(s)
    cos_b, sin_b = _rope_tables_axial(s)
    ones = jnp.ones((HEAD_DIM,), F32)
    x2 = x.reshape(n, d)
    for i in range(depth):
        j = i // N_MIXERS
        if i % N_MIXERS == 0:
            qkv, vt = _qkv(x2, norm_mix[i], a_w_qkv[j], cos_a, sin_a, ones, ones,
                           seq_len=s, qk_norm=False, axial=False)
            vt = vt.reshape(b, -1, vt.shape[1], vt.shape[2])
            sink_rows = jnp.repeat(a_sink[j].astype(F32), WINDOW).reshape(-1, 1, GROUP * WINDOW)
            o = _attn_window(qkv.reshape(b, s, qkv_dim), vt, sink_rows, d_model=d)
            x2 = _oproj(o.reshape(n, d), a_w_o[j], x2)
        else:
            qkv, vt = _qkv(x2, norm_mix[i], b_w_qkv[j], cos_b, sin_b, b_q_norm[j], b_k_norm[j],
                           seq_len=s, qk_norm=True, axial=True, q_scale=DENSE_Q_SCALE)
            vt = vt.reshape(b, -1, vt.shape[1], vt.shape[2])
            o = _attn_dense(qkv.reshape(b, s, qkv_dim), vt, d_model=d)
            x2 = _oproj(o.reshape(n, d), b_w_o[j], x2)
        hn, aff_t = _norm_router(x2, norm_ffn[i], wr_t[i])
        idx, gate = _select(aff_t)
        x2 = _moe(x2, hn, idx, gate, w_gate, w_up, w_down, i)
    return _final_norm(x2, final_norm).reshape(b, s, d)


def kernel(x_prompt, x_sample, norm_mix, norm_ffn, final_norm, a_w_qkv, a_w_o, a_sink, b_w_qkv, b_w_o,
           b_q_norm, b_k_norm, w_router, w_gate, w_up, w_down):
    def experts(w):
        return w.astype(BF16).reshape((-1,) + w.shape[2:])

    weights = (a_w_qkv.astype(BF16), a_w_o.astype(BF16), a_sink, b_w_qkv.astype(BF16), b_w_o.astype(BF16),
               b_q_norm, b_k_norm, jnp.swapaxes(w_router, 1, 2).astype(BF16),
               experts(w_gate), experts(w_up), experts(w_down))
    y_prompt = _trunk(x_prompt, norm_mix, norm_ffn, final_norm, *weights)
    y_sample = _trunk(x_sample, norm_mix, norm_ffn, final_norm, *weights)
    return (y_prompt, y_sample)
```

```python
import functools
import math

import jax
import jax.numpy as jnp
from jax import lax
from jax.experimental import pallas as pl
from jax.experimental.pallas import tpu as pltpu

HEAD_DIM = 128
GROUP = 4
WINDOW = 128
ROPE_THETA = 10000.0
GRID_W = 64
N_EXPERTS = 16
CAPACITY_FACTOR = 2
N_MIXERS = 2
EPS = 1e-6
NEG = -1e30
LANES = 128
SEQ_TILE = 512
MOE_STEPS = 8
ONES_ROWS = 16
DENSE_Q_SCALE = math.log2(math.e) / math.sqrt(HEAD_DIM)
V7X_VMEM_LIMIT_BYTES = 56 * 1024 * 1024

BF16 = jnp.bfloat16
F32 = jnp.float32
_NT = (((1,), (1,)), ((), ()))


def _params(*sem):
    return pltpu.CompilerParams(dimension_semantics=sem, vmem_limit_bytes=V7X_VMEM_LIMIT_BYTES)


def _tile(n, target):
    t = min(n, target)
    while n % t:
        t -= 1
    return t


def _rms(x, g):
    return x * lax.rsqrt(jnp.mean(x * x, axis=-1, keepdims=True) + EPS) * g


def _qkv_kernel(x_ref, g_ref, w_ref, cos_ref, sin_ref, gain_ref, o_ref, vt_ref, hn_ref, *,
                n_qk_tiles, qk_norm, axial, cols):
    j = pl.program_id(1)
    is_v = j >= n_qk_tiles

    @pl.when(j == 0)
    def _():
        hn_ref[...] = _rms(x_ref[...], g_ref[...]).astype(BF16)

    tm, tn = o_ref.shape
    cos = cos_ref[0]
    sin = sin_ref[0]
    gain = gain_ref[0]
    if axial:
        lane = lax.broadcasted_iota(jnp.int32, (tm, HEAD_DIM), 1)
        low_half = (lane % (HEAD_DIM // 2)) < (HEAD_DIM // 4)
    outs = []
    for cb in range(tn // cols):
        acc = jnp.dot(hn_ref[...], w_ref[:, cb * cols:(cb + 1) * cols], preferred_element_type=F32)
        heads = []
        for h in range(cols // HEAD_DIM):
            a = acc[:, h * HEAD_DIM:(h + 1) * HEAD_DIM]
            if qk_norm:
                a = jnp.where(is_v, a, _rms(a, gain))
            if axial:
                partner = jnp.where(low_half, pltpu.roll(a, 3 * HEAD_DIM // 4, 1),
                                    pltpu.roll(a, HEAD_DIM // 4, 1))
            else:
                partner = pltpu.roll(a, HEAD_DIM // 2, 1)
            heads.append(a * cos + partner * sin)
        out = jnp.concatenate(heads, axis=1)
        o_ref[:, cb * cols:(cb + 1) * cols] = out.astype(BF16)
        outs.append(out)

    @pl.when(is_v)
    def _():
        vt_ref[0] = jnp.concatenate(outs, axis=1).T.astype(BF16)


def _qkv(x2, g, w, cos3, sin3, gain3, *, seq_len, qk_norm, axial):
    n, d = x2.shape
    qkv_dim = w.shape[1]
    kv_dim = (qkv_dim - d) // 2
    tm = _tile(seq_len, SEQ_TILE)
    tn = _tile(kv_dim, 1024)
    s_tiles = seq_len // tm
    n_q_tiles = d // tn
    n_qk_tiles = (d + kv_dim) // tn
    kern = functools.partial(_qkv_kernel, n_qk_tiles=n_qk_tiles, qk_norm=qk_norm, axial=axial,
                             cols=_tile(tn, 2 * HEAD_DIM))

    def kind(j):
        return (j >= n_q_tiles).astype(jnp.int32) + (j >= n_qk_tiles).astype(jnp.int32)

    return pl.pallas_call(
        kern,
        grid=(n // tm, qkv_dim // tn),
        in_specs=[
            pl.BlockSpec((tm, d), lambda i, j: (i, 0)),
            pl.BlockSpec((1, d), lambda i, j: (0, 0)),
            pl.BlockSpec((d, tn), lambda i, j: (0, j)),
            pl.BlockSpec((1, tm, HEAD_DIM), lambda i, j: (kind(j), i % s_tiles, 0)),
            pl.BlockSpec((1, tm, HEAD_DIM), lambda i, j: (kind(j), i % s_tiles, 0)),
            pl.BlockSpec((1, 1, HEAD_DIM), lambda i, j: (kind(j), 0, 0)),
        ],
        out_specs=[pl.BlockSpec((tm, tn), lambda i, j: (i, j)),
                   pl.BlockSpec((1, tn, tm), lambda i, j: (i, jnp.maximum(j - n_qk_tiles, 0), 0))],
        out_shape=[jax.ShapeDtypeStruct((n, qkv_dim), BF16),
                   jax.ShapeDtypeStruct((n // tm, kv_dim, tm), BF16)],
        scratch_shapes=[pltpu.VMEM((tm, d), BF16)],
        compiler_params=_params("parallel", "arbitrary"),
        name="qkv_proj",
    )(x2, g.reshape(1, d), w, cos3, sin3, gain3)


def _heads_to_lanes(q):
    qf = q.astype(F32)
    return jnp.concatenate([qf[:, h * HEAD_DIM:(h + 1) * HEAD_DIM].T for h in range(GROUP)],
                           axis=1).astype(BF16)


def _lanes_to_heads(ot):
    t = ot.shape[1] // GROUP
    return jnp.concatenate([ot[:, h * t:(h + 1) * t].T for h in range(GROUP)], axis=1)


def _attn_window_kernel(q_ref, kp_ref, kc_ref, kn_ref, vp_ref, vc_ref, vn_ref, sink_ref, bias_ref, o_ref):
    log2e = math.log2(math.e)
    qw = GROUP * HEAD_DIM
    n_kv = q_ref.shape[2] // qw
    bias = bias_ref[0]

    def head_slice(ref, g):
        return ref[0, :, g * HEAD_DIM:(g + 1) * HEAD_DIM]

    def vt_slice(ref, g):
        return ref[0, 0, g * HEAD_DIM:(g + 1) * HEAD_DIM, :]

    sts = []
    for g in range(n_kv):
        qt = _heads_to_lanes(q_ref[0, :, g * qw:(g + 1) * qw])
        k = jnp.concatenate([head_slice(kp_ref, g), head_slice(kc_ref, g), head_slice(kn_ref, g)], axis=0)
        sts.append(jnp.dot(k, qt, preferred_element_type=F32))
    for g in range(n_kv):
        st = sts[g] * (log2e / math.sqrt(HEAD_DIM)) + bias
        sk = sink_ref[g] * log2e
        m = jnp.maximum(jnp.max(st, axis=0, keepdims=True), sk)
        p = jnp.exp2(st - m)
        denom = jnp.sum(p, axis=0, keepdims=True) + jnp.exp2(sk - m)
        vt = jnp.concatenate([vt_slice(vp_ref, g), vt_slice(vc_ref, g), vt_slice(vn_ref, g)], axis=1)
        ot = jnp.dot(vt, p.astype(BF16), preferred_element_type=F32) / denom
        o_ref[0, :, g * qw:(g + 1) * qw] = _lanes_to_heads(ot).astype(BF16)


def _window_bias():
    blk = WINDOW
    c = jnp.arange(3 * blk)[:, None]
    r = (jnp.arange(GROUP * blk) % blk)[None, :]
    band = jnp.abs(c - r - blk) <= WINDOW
    variants = []
    for last in (False, True):
        for first in (False, True):
            ok = band
            if first:
                ok = ok & (c >= blk)
            if last:
                ok = ok & (c < 2 * blk)
            variants.append(jnp.where(ok, 0.0, NEG).astype(F32))
    return jnp.stack([variants[0], variants[1], variants[2], variants[3]])


def _attn_window(qkv, vt, sink_rows, *, d_model):
    b, s, qkv_dim = qkv.shape
    n_kv = (qkv_dim - d_model) // (2 * HEAD_DIM)
    blk = WINDOW
    nb = s // blk
    sub = vt.shape[3] // blk
    kv_dim = n_kv * HEAD_DIM
    kcol = d_model // kv_dim
    bias = _window_bias()

    def k_spec(shift):
        return pl.BlockSpec((1, blk, kv_dim), lambda bi, i: (bi, jnp.clip(i + shift, 0, nb - 1), kcol))

    def vt_spec(shift):
        def index(bi, i):
            kb = jnp.clip(i + shift, 0, nb - 1)
            return (bi, kb // sub, 0, kb % sub)
        return pl.BlockSpec((1, 1, kv_dim, blk), index)

    def bias_index(bi, i):
        return ((i == 0).astype(jnp.int32) + 2 * (i == nb - 1).astype(jnp.int32), 0, 0)

    return pl.pallas_call(
        _attn_window_kernel,
        grid=(b, nb),
        in_specs=[pl.BlockSpec((1, blk, d_model), lambda bi, i: (bi, i, 0)),
                  k_spec(-1), k_spec(0), k_spec(1), vt_spec(-1), vt_spec(0), vt_spec(1),
                  pl.BlockSpec((n_kv, 1, GROUP * blk), lambda bi, i: (0, 0, 0)),
                  pl.BlockSpec((1, 3 * blk, GROUP * blk), bias_index)],
        out_specs=pl.BlockSpec((1, blk, d_model), lambda bi, i: (bi, i, 0)),
        out_shape=jax.ShapeDtypeStruct((b, s, d_model), BF16),
        compiler_params=_params("parallel", "parallel"),
        name="attn_window",
    )(qkv, qkv, qkv, qkv, vt, vt, vt, sink_rows, bias)


def _attn_dense_kernel(q_ref, k_ref, vt_ref, o_ref, qt_ref, acc0_ref, acc1_ref, acc2_ref, acc3_ref, *, unroll):
    tk = vt_ref.shape[3]
    qt_ref[...] = _heads_to_lanes(q_ref[0])
    for acc_ref in (acc0_ref, acc1_ref, acc2_ref, acc3_ref):
        acc_ref[...] = jnp.zeros(acc_ref.shape, F32)

    tq = qt_ref.shape[1] // GROUP
    accs = (acc0_ref, acc1_ref, acc2_ref, acc3_ref)
    ones = jnp.ones((ONES_ROWS, tk), BF16)

    def scores(c):
        off = pl.multiple_of(c * tk, tk)
        k = k_ref[0, pl.ds(off, tk), :]
        return [jnp.dot(k, qt_ref[:, h * tq:(h + 1) * tq], preferred_element_type=F32)
                for h in range(GROUP)]

    def update(c, h, st, m_prev):
        m_new = jnp.maximum(m_prev, jnp.max(st, axis=0, keepdims=True))
        p = jnp.exp2(st - m_new).astype(BF16)
        vt1 = jnp.concatenate([vt_ref[0, c], ones], axis=0)
        accs[h][...] = jnp.exp2(m_prev - m_new) * accs[h][...] + jnp.dot(vt1, p, preferred_element_type=F32)
        return m_new

    def body(cc, ms):
        ms = list(ms)
        c0 = cc * unroll
        st = scores(c0)
        for u in range(unroll):
            st_next = [None] * GROUP
            for h in range(GROUP):
                ms[h] = update(c0 + u, h, st[h], ms[h])
                if u + 1 < unroll and h == 0:
                    st_next = scores(c0 + u + 1)
            st = st_next
        return tuple(ms)

    init = tuple(jnp.full((1, tq), NEG, F32) for _ in range(GROUP))
    lax.fori_loop(0, vt_ref.shape[1] // unroll, body, init)
    ot = jnp.concatenate([accs[h][0:HEAD_DIM] / accs[h][HEAD_DIM:HEAD_DIM + 1] for h in range(GROUP)], axis=1)
    o_ref[0] = _lanes_to_heads(ot).astype(BF16)


def _attn_dense(qkv, vt, *, d_model):
    b, s, qkv_dim = qkv.shape
    n_kv = (qkv_dim - d_model) // (2 * HEAD_DIM)
    tq = _tile(s, 256)
    n_kt, tk = vt.shape[1], vt.shape[3]
    kcol = d_model // HEAD_DIM
    qw = GROUP * HEAD_DIM
    return pl.pallas_call(
        functools.partial(_attn_dense_kernel, unroll=_tile(n_kt, 8)),
        grid=(b, n_kv, s // tq),
        in_specs=[pl.BlockSpec((1, tq, qw), lambda bi, g, i: (bi, i, g)),
                  pl.BlockSpec((1, s, HEAD_DIM), lambda bi, g, i: (bi, 0, kcol + g)),
                  pl.BlockSpec((1, n_kt, HEAD_DIM, tk), lambda bi, g, i: (bi, 0, g, 0))],
        out_specs=pl.BlockSpec((1, tq, qw), lambda bi, g, i: (bi, i, g)),
        out_shape=jax.ShapeDtypeStruct((b, s, d_model), BF16),
        scratch_shapes=[pltpu.VMEM((HEAD_DIM, GROUP * tq), BF16)]
        + [pltpu.VMEM((HEAD_DIM + ONES_ROWS, tq), F32) for _ in range(GROUP)],
        compiler_params=_params("parallel", "parallel", "parallel"),
        name="attn_dense",
    )(qkv, qkv, vt)


def _oproj_kernel(o_ref, w_ref, x_ref, out_ref):
    out_ref[...] = x_ref[...] + jnp.dot(o_ref[...], w_ref[...], preferred_element_type=F32)


def _oproj(o2, w, x2):
    n, q_dim = o2.shape
    d = w.shape[1]
    tm = _tile(n, 512)
    tn = _tile(d, 1024)
    return pl.pallas_call(
        _oproj_kernel,
        grid=(n // tm, d // tn),
        in_specs=[pl.BlockSpec((tm, q_dim), lambda i, j: (i, 0)),
                  pl.BlockSpec((q_dim, tn), lambda i, j: (0, j)),
                  pl.BlockSpec((tm, tn), lambda i, j: (i, j))],
        out_specs=pl.BlockSpec((tm, tn), lambda i, j: (i, j)),
        out_shape=jax.ShapeDtypeStruct((n, d), F32),
        compiler_params=_params("parallel", "parallel"),
        name="o_proj",
    )(o2, w, x2)


def _norm_router_kernel(x_ref, g_ref, wr_ref, hn_ref, aff_ref):
    hn = _rms(x_ref[...], g_ref[...])
    hn_ref[...] = hn
    logits = lax.dot_general(wr_ref[...], hn.astype(BF16), _NT, preferred_element_type=F32)
    e = jnp.exp(logits - jnp.max(logits, axis=0, keepdims=True))
    aff_ref[...] = e / jnp.sum(e, axis=0, keepdims=True)


def _norm_router(x2, g, wr_t):
    n, d = x2.shape
    tm = _tile(n, 256)
    return pl.pallas_call(
        _norm_router_kernel,
        grid=(n // tm,),
        in_specs=[pl.BlockSpec((tm, d), lambda i: (i, 0)),
                  pl.BlockSpec((1, d), lambda i: (0, 0)),
                  pl.BlockSpec((N_EXPERTS, d), lambda i: (0, 0))],
        out_specs=[pl.BlockSpec((tm, d), lambda i: (i, 0)),
                   pl.BlockSpec((N_EXPERTS, tm), lambda i: (0, i))],
        out_shape=[jax.ShapeDtypeStruct((n, d), F32), jax.ShapeDtypeStruct((N_EXPERTS, n), F32)],
        compiler_params=_params("parallel"),
        name="norm_router",
    )(x2, g.reshape(1, d), wr_t)


def _final_norm_kernel(x_ref, g_ref, o_ref):
    o_ref[...] = _rms(x_ref[...], g_ref[...])


def _final_norm(x2, g):
    n, d = x2.shape
    tm = _tile(n, 256)
    return pl.pallas_call(
        _final_norm_kernel,
        grid=(n // tm,),
        in_specs=[pl.BlockSpec((tm, d), lambda i: (i, 0)), pl.BlockSpec((1, d), lambda i: (0, 0))],
        out_specs=pl.BlockSpec((tm, d), lambda i: (i, 0)),
        out_shape=jax.ShapeDtypeStruct((n, d), F32),
        compiler_params=_params("parallel"),
        name="final_norm",
    )(x2, g.reshape(1, d))


def _select_kernel(aff_ref, idx_ref, gate_ref, *, cap):
    a = aff_ref[0]
    nb = a.shape[0]
    bits = lax.bitcast_convert_type(a, jnp.int32)

    def count(mask):
        return jnp.sum(jnp.sum(jnp.where(mask, 1.0, 0.0), axis=1, keepdims=True), axis=0, keepdims=True)

    thr = jnp.zeros((1, 1), jnp.int32)
    for bit in range(30, -1, -1):
        cand = thr | (1 << bit)
        thr = jnp.where(count(bits >= cand) >= cap, cand, thr)

    tri_u = jnp.where(lax.broadcasted_iota(jnp.int32, (LANES, LANES), 0)
                      <= lax.broadcasted_iota(jnp.int32, (LANES, LANES), 1), 1.0, 0.0).astype(BF16)
    rk = lax.broadcasted_iota(jnp.int32, (nb, nb), 0)
    ck = lax.broadcasted_iota(jnp.int32, (nb, nb), 1)
    tri_strict_l = jnp.where(ck < rk, 1.0, 0.0).astype(BF16)
    tri_u_nb = jnp.where(rk <= ck, 1.0, 0.0).astype(BF16)

    def local_cumsum(maskf):
        return jnp.dot(maskf.astype(BF16), tri_u, preferred_element_type=F32)

    gt = bits > thr
    eq = bits == thr
    eqf = jnp.where(eq, 1.0, 0.0)
    need = cap - count(gt)
    eq_local = local_cumsum(eqf)
    eq_tot = jnp.broadcast_to(eq_local[:, LANES - 1:LANES], (nb, LANES))
    eq_before = jnp.dot(tri_strict_l, eq_tot.astype(BF16), preferred_element_type=F32)
    eq_rank = eq_before + eq_local - eqf
    sel = gt | (eq & (eq_rank < need))
    self_ = jnp.where(sel, 1.0, 0.0)

    cl = local_cumsum(self_)
    ones8 = jnp.ones((8, LANES), BF16)
    cnt_row = lax.dot_general(ones8, self_.astype(BF16), _NT, preferred_element_type=F32)
    bend_row = jnp.dot(cnt_row.astype(BF16), tri_u_nb, preferred_element_type=F32)[0:1]
    bprev_row = bend_row - cnt_row[0:1]

    p = lax.broadcasted_iota(jnp.int32, (cap, nb), 0).astype(F32)
    hot_blk = jnp.where((bprev_row <= p) & (p < bend_row), 1.0, 0.0)
    kcol = lax.broadcasted_iota(jnp.int32, (cap, nb), 1).astype(F32)
    blk_id = jnp.sum(hot_blk * kcol, axis=1, keepdims=True)
    p_local = p[:, 0:1] - jnp.sum(hot_blk * bprev_row, axis=1, keepdims=True)
    hot_b = hot_blk.astype(BF16)
    row_cl = jnp.dot(hot_b, cl.astype(BF16), preferred_element_type=F32)
    row_sel = jnp.dot(hot_b, self_.astype(BF16), preferred_element_type=F32)
    hot_tok = jnp.where((row_sel > 0.5) & (row_cl == p_local + 1.0), 1.0, 0.0)
    jcol = lax.broadcasted_iota(jnp.int32, (cap, LANES), 1).astype(F32)
    tok = blk_id * float(LANES) + jnp.sum(hot_tok * jcol, axis=1, keepdims=True)
    a1 = a.astype(BF16)
    r1 = a - a1.astype(F32)
    a2 = r1.astype(BF16)
    a3 = (r1 - a2.astype(F32)).astype(BF16)
    row_a = (jnp.dot(hot_b, a1, preferred_element_type=F32) + jnp.dot(hot_b, a2, preferred_element_type=F32)
             + jnp.dot(hot_b, a3, preferred_element_type=F32))
    idx_ref[0] = tok.astype(jnp.int32)
    gate_ref[0] = jnp.sum(hot_tok * row_a, axis=1, keepdims=True)


def _select(aff_t):
    e, n = aff_t.shape
    cap = CAPACITY_FACTOR * n // N_EXPERTS
    nb = n // LANES
    idx, gate = pl.pallas_call(
        functools.partial(_select_kernel, cap=cap),
        grid=(e,),
        in_specs=[pl.BlockSpec((1, nb, LANES), lambda i: (i, 0, 0))],
        out_specs=[pl.BlockSpec((1, cap, 1), lambda i: (i, 0, 0)),
                   pl.BlockSpec((1, cap, 1), lambda i: (i, 0, 0))],
        out_shape=[jax.ShapeDtypeStruct((e, cap, 1), jnp.int32), jax.ShapeDtypeStruct((e, cap, 1), F32)],
        compiler_params=_params("parallel"),
        name="expert_select",
    )(aff_t.reshape(e, nb, LANES))
    return idx.reshape(e * cap), gate.reshape(e * cap, 1)


def _moe_kernel(idx_ref, hn_hbm, gate_ref, wg_ref, wu_ref, wd_ref, x_in_hbm, x_hbm,
                rows, xe_ref, acc_ref, hn_sem, xin_sem, xout_sem, *, tm, tiles_per_expert, n_tiles, dn):
    del x_in_hbm
    f = pl.program_id(2)
    t = pl.program_id(0) * tiles_per_expert + pl.program_id(1)
    slot = t % 2
    other = 1 - slot
    half = tm // 2

    def hn_copy(tile, slot_):
        return lambda row: pltpu.make_async_copy(
            hn_hbm.at[pl.ds(idx_ref[tile * tm + row], 1)], rows.at[slot_, pl.ds(row, 1)], hn_sem.at[slot_])

    def x_in_copy(row):
        return pltpu.make_async_copy(
            x_hbm.at[pl.ds(idx_ref[t * tm + row], 1)], rows.at[slot, pl.ds(row, 1)], xin_sem)

    def x_out_copy(tile, slot_):
        return lambda row: pltpu.make_async_copy(
            rows.at[slot_, pl.ds(row, 1)], x_hbm.at[pl.ds(idx_ref[tile * tm + row], 1)], xout_sem)

    def start_all(make):
        def body(r, carry):
            make(r).start()
            return carry
        lax.fori_loop(0, tm, body, 0, unroll=8)

    def start_half(make, part):
        for r in range(half):
            make(part * half + r).start()

    def wait_hn():
        pltpu.make_async_copy(hn_hbm.at[pl.ds(0, tm)], rows.at[slot], hn_sem.at[slot]).wait()

    def wait_x_in():
        pltpu.make_async_copy(x_hbm.at[pl.ds(0, tm)], rows.at[slot], xin_sem).wait()

    def wait_x_out():
        pltpu.make_async_copy(rows.at[slot], x_hbm.at[pl.ds(0, tm)], xout_sem).wait()

    @pl.when(f == 0)
    def _():
        @pl.when(t == 0)
        def _():
            start_all(hn_copy(0, 0))
        wait_hn()
        xe_ref[...] = rows[slot].astype(BF16)
        acc_ref[...] = jnp.zeros(acc_ref.shape, F32)

    @pl.when((f == 3) & (t > 0))
    def _():
        wait_x_out()

    def step(issue=None):
        if issue is not None:
            issue()
        xe = xe_ref[...]
        g = jnp.dot(xe, wg_ref[0], preferred_element_type=F32)
        u = jnp.dot(xe, wu_ref[0], preferred_element_type=F32)
        h = (g * jax.nn.sigmoid(g) * u).astype(BF16)
        for c in range(acc_ref.shape[1] // dn):
            cols = slice(c * dn, (c + 1) * dn)
            acc_ref[:, cols] += jnp.dot(h, wd_ref[0, :, cols], preferred_element_type=F32)

    issue_out = (f <= 1) & (t > 0)
    issue_x_in = (f == 3) | (f == 4)
    issue_hn = ((f == 5) | (f == 6)) & (t + 1 < n_tiles)

    @pl.when(issue_out)
    def _():
        step(lambda: start_half(x_out_copy(t - 1, other), f))

    @pl.when(issue_x_in)
    def _():
        step(lambda: start_half(x_in_copy, f - 3))

    @pl.when(issue_hn)
    def _():
        step(lambda: start_half(hn_copy(t + 1, other), f - 5))

    @pl.when(jnp.logical_not(issue_out | issue_x_in | issue_hn))
    def _():
        step()

    @pl.when(f == MOE_STEPS - 1)
    def _():
        wait_x_in()
        rows[slot] = rows[slot] + acc_ref[...] * gate_ref[...]

        @pl.when(t == n_tiles - 1)
        def _():
            start_all(x_out_copy(t, slot))
            wait_x_out()


def _moe(x2, hn, idx, gate, wg, wu, wd, layer):
    n, d = x2.shape
    e = N_EXPERTS
    d_exp = wg.shape[2]
    e0 = layer * e
    cap = idx.shape[0] // e
    tm = _tile(cap, 512)
    assert d_exp % MOE_STEPS == 0 and tm % 2 == 0
    tf = d_exp // MOE_STEPS
    tiles_per_expert = cap // tm
    kern = functools.partial(_moe_kernel, tm=tm, tiles_per_expert=tiles_per_expert,
                             n_tiles=e * tiles_per_expert, dn=_tile(d, 1024))
    grid_spec = pltpu.PrefetchScalarGridSpec(
        num_scalar_prefetch=1,
        grid=(e, tiles_per_expert, d_exp // tf),
        in_specs=[
            pl.BlockSpec(memory_space=pl.ANY),
            pl.BlockSpec((tm, 1), lambda ei, i, f, idx_: (ei * tiles_per_expert + i, 0)),
            pl.BlockSpec((1, d, tf), lambda ei, i, f, idx_: (e0 + ei, 0, f)),
            pl.BlockSpec((1, d, tf), lambda ei, i, f, idx_: (e0 + ei, 0, f)),
            pl.BlockSpec((1, tf, d), lambda ei, i, f, idx_: (e0 + ei, f, 0)),
            pl.BlockSpec(memory_space=pl.ANY),
        ],
        out_specs=pl.BlockSpec(memory_space=pl.ANY),
        scratch_shapes=[
            pltpu.VMEM((2, tm, d), F32),
            pltpu.VMEM((tm, d), BF16),
            pltpu.VMEM((tm, d), F32),
            pltpu.SemaphoreType.DMA((2,)),
            pltpu.SemaphoreType.DMA(()),
            pltpu.SemaphoreType.DMA(()),
        ],
    )
    return pl.pallas_call(
        kern,
        grid_spec=grid_spec,
        out_shape=jax.ShapeDtypeStruct((n, d), F32),
        input_output_aliases={6: 0},
        compiler_params=_params("arbitrary", "arbitrary", "arbitrary"),
        name="moe_experts",
    )(idx, hn, gate, wg, wu, wd, x2)


def _angles(pos, dim):
    inv_freq = ROPE_THETA ** (-jnp.arange(0, dim, 2, dtype=F32) / dim)
    return pos.astype(F32)[:, None] * inv_freq[None, :]


def _rope_tables_1d(s):
    ang = _angles(jnp.arange(s), HEAD_DIM)
    cos, sin = jnp.cos(ang), jnp.sin(ang)
    return jnp.concatenate([cos, cos], axis=1), jnp.concatenate([-sin, sin], axis=1)


def _rope_tables_axial(s):
    t = jnp.arange(s)
    half = HEAD_DIM // 2
    ar, ac = _angles(t // GRID_W, half), _angles(t % GRID_W, half)
    cos = jnp.concatenate([jnp.cos(ar), jnp.cos(ar), jnp.cos(ac), jnp.cos(ac)], axis=1)
    sin = jnp.concatenate([-jnp.sin(ar), jnp.sin(ar), -jnp.sin(ac), jnp.sin(ac)], axis=1)
    return cos, sin


def _trunk(x, norm_mix, norm_ffn, final_norm, a_w_qkv, a_w_o, a_sink, b_w_qkv, b_w_o, b_q_norm, b_k_norm,
           wr_t, w_gate, w_up, w_down):
    b, s, d = x.shape
    n = b * s
    depth = norm_mix.shape[0]
    qkv_dim = a_w_qkv.shape[2]
    def by_kind(cos, sin, q_scale):
        return (jnp.stack([cos * q_scale, cos, jnp.ones_like(cos)]),
                jnp.stack([sin * q_scale, sin, jnp.zeros_like(sin)]))

    cos_a, sin_a = by_kind(*_rope_tables_1d(s), 1.0)
    cos_b, sin_b = by_kind(*_rope_tables_axial(s), DENSE_Q_SCALE)
    ones = jnp.ones((HEAD_DIM,), F32)
    gain_a = jnp.stack([ones, ones, ones]).reshape(3, 1, HEAD_DIM)
    x2 = x.reshape(n, d)
    for i in range(depth):
        j = i // N_MIXERS
        if i % N_MIXERS == 0:
            qkv, vt = _qkv(x2, norm_mix[i], a_w_qkv[j], cos_a, sin_a, gain_a,
                           seq_len=s, qk_norm=False, axial=False)
            vt = vt.reshape(b, -1, vt.shape[1], vt.shape[2])
            sink_rows = jnp.repeat(a_sink[j].astype(F32), WINDOW).reshape(-1, 1, GROUP * WINDOW)
            o = _attn_window(qkv.reshape(b, s, qkv_dim), vt, sink_rows, d_model=d)
            x2 = _oproj(o.reshape(n, d), a_w_o[j], x2)
        else:
            gain_b = jnp.stack([b_q_norm[j].astype(F32), b_k_norm[j].astype(F32), ones]).reshape(3, 1, HEAD_DIM)
            qkv, vt = _qkv(x2, norm_mix[i], b_w_qkv[j], cos_b, sin_b, gain_b,
                           seq_len=s, qk_norm=True, axial=True)
            vt = vt.reshape(b, -1, vt.shape[1], vt.shape[2])
            o = _attn_dense(qkv.reshape(b, s, qkv_dim), vt, d_model=d)
            x2 = _oproj(o.reshape(n, d), b_w_o[j], x2)
        hn, aff_t = _norm_router(x2, norm_ffn[i], wr_t[i])
        idx, gate = _select(aff_t)
        x2 = _moe(x2, hn, idx, gate, w_gate, w_up, w_down, i)
    return _final_norm(x2, final_norm).reshape(b, s, d)


def kernel(x_prompt, x_sample, norm_mix, norm_ffn, final_norm, a_w_qkv, a_w_o, a_sink, b_w_qkv, b_w_o,
           b_q_norm, b_k_norm, w_router, w_gate, w_up, w_down):
    def experts(w):
        return w.astype(BF16).reshape((-1,) + w.shape[2:])

    weights = (a_w_qkv.astype(BF16), a_w_o.astype(BF16), a_sink, b_w_qkv.astype(BF16), b_w_o.astype(BF16),
               b_q_norm, b_k_norm, jnp.swapaxes(w_router, 1, 2).astype(BF16),
               experts(w_gate), experts(w_up), experts(w_down))
    y_prompt = _trunk(x_prompt, norm_mix, norm_ffn, final_norm, *weights)
    y_sample = _trunk(x_sample, norm_mix, norm_ffn, final_norm, *weights)
    return (y_prompt, y_sample)
```

```python
import functools
import math

import jax
import jax.numpy as jnp
from jax import lax
from jax.experimental import pallas as pl
from jax.experimental.pallas import tpu as pltpu

HEAD_DIM = 128
GROUP = 4
WINDOW = 128
ROPE_THETA = 10000.0
GRID_W = 64
N_EXPERTS = 16
CAPACITY_FACTOR = 2
N_MIXERS = 2
EPS = 1e-6
NEG = -1e30
LANES = 128
SEQ_TILE = 512
MOE_STEPS = 8
ONES_ROWS = 16
DENSE_Q_SCALE = math.log2(math.e) / math.sqrt(HEAD_DIM)
V7X_VMEM_LIMIT_BYTES = 56 * 1024 * 1024

BF16 = jnp.bfloat16
F32 = jnp.float32
_NT = (((1,), (1,)), ((), ()))


def _params(*sem):
    return pltpu.CompilerParams(dimension_semantics=sem, vmem_limit_bytes=V7X_VMEM_LIMIT_BYTES)


def _tile(n, target):
    t = min(n, target)
    while n % t:
        t -= 1
    return t


def _rms(x, g):
    return x * lax.rsqrt(jnp.mean(x * x, axis=-1, keepdims=True) + EPS) * g


def _qkv_kernel(x_ref, g_ref, w_ref, cos_ref, sin_ref, gain_ref, o_ref, vt_ref, hn_ref, *,
                n_qk_tiles, qk_norm, axial, cols):
    j = pl.program_id(1)
    is_v = j >= n_qk_tiles

    @pl.when(j == 0)
    def _():
        hn_ref[...] = _rms(x_ref[...], g_ref[...]).astype(BF16)

    tm, tn = o_ref.shape
    cos = cos_ref[0]
    sin = sin_ref[0]
    gain = gain_ref[0]
    if axial:
        lane = lax.broadcasted_iota(jnp.int32, (tm, HEAD_DIM), 1)
        low_half = (lane % (HEAD_DIM // 2)) < (HEAD_DIM // 4)
    outs = []
    for cb in range(tn // cols):
        acc = jnp.dot(hn_ref[...], w_ref[:, cb * cols:(cb + 1) * cols], preferred_element_type=F32)
        heads = []
        for h in range(cols // HEAD_DIM):
            a = acc[:, h * HEAD_DIM:(h + 1) * HEAD_DIM]
            if qk_norm:
                a = jnp.where(is_v, a, _rms(a, gain))
            if axial:
                partner = jnp.where(low_half, pltpu.roll(a, 3 * HEAD_DIM // 4, 1),
                                    pltpu.roll(a, HEAD_DIM // 4, 1))
            else:
                partner = pltpu.roll(a, HEAD_DIM // 2, 1)
            heads.append(a * cos + partner * sin)
        out = jnp.concatenate(heads, axis=1)
        o_ref[:, cb * cols:(cb + 1) * cols] = out.astype(BF16)
        outs.append(out)

    @pl.when(is_v)
    def _():
        vt_ref[0] = jnp.concatenate(outs, axis=1).T.astype(BF16)


def _qkv(x2, g, w, cos3, sin3, gain3, *, seq_len, qk_norm, axial):
    n, d = x2.shape
    qkv_dim = w.shape[1]
    kv_dim = (qkv_dim - d) // 2
    tm = _tile(seq_len, SEQ_TILE)
    tn = _tile(kv_dim, 1024)
    s_tiles = seq_len // tm
    n_q_tiles = d // tn
    n_qk_tiles = (d + kv_dim) // tn
    kern = functools.partial(_qkv_kernel, n_qk_tiles=n_qk_tiles, qk_norm=qk_norm, axial=axial,
                             cols=_tile(tn, 2 * HEAD_DIM))

    def kind(j):
        return (j >= n_q_tiles).astype(jnp.int32) + (j >= n_qk_tiles).astype(jnp.int32)

    return pl.pallas_call(
        kern,
        grid=(n // tm, qkv_dim // tn),
        in_specs=[
            pl.BlockSpec((tm, d), lambda i, j: (i, 0)),
            pl.BlockSpec((1, d), lambda i, j: (0, 0)),
            pl.BlockSpec((d, tn), lambda i, j: (0, j)),
            pl.BlockSpec((1, tm, HEAD_DIM), lambda i, j: (kind(j), i % s_tiles, 0)),
            pl.BlockSpec((1, tm, HEAD_DIM), lambda i, j: (kind(j), i % s_tiles, 0)),
            pl.BlockSpec((1, 1, HEAD_DIM), lambda i, j: (kind(j), 0, 0)),
        ],
        out_specs=[pl.BlockSpec((tm, tn), lambda i, j: (i, j)),
                   pl.BlockSpec((1, tn, tm), lambda i, j: (i, jnp.maximum(j - n_qk_tiles, 0), 0))],
        out_shape=[jax.ShapeDtypeStruct((n, qkv_dim), BF16),
                   jax.ShapeDtypeStruct((n // tm, kv_dim, tm), BF16)],
        scratch_shapes=[pltpu.VMEM((tm, d), BF16)],
        compiler_params=_params("parallel", "arbitrary"),
        name="qkv_proj",
    )(x2, g.reshape(1, d), w, cos3, sin3, gain3)


def _heads_to_lanes(q):
    qf = q.astype(F32)
    return jnp.concatenate([qf[:, h * HEAD_DIM:(h + 1) * HEAD_DIM].T for h in range(GROUP)],
                           axis=1).astype(BF16)


def _lanes_to_heads(ot):
    t = ot.shape[1] // GROUP
    return jnp.concatenate([ot[:, h * t:(h + 1) * t].T for h in range(GROUP)], axis=1)


def _attn_window_kernel(q_ref, kp_ref, kc_ref, kn_ref, vp_ref, vc_ref, vn_ref, sink_ref, bias_ref, o_ref):
    log2e = math.log2(math.e)
    qw = GROUP * HEAD_DIM
    n_kv = q_ref.shape[2] // qw
    bias = bias_ref[0]

    def head_slice(ref, g):
        return ref[0, :, g * HEAD_DIM:(g + 1) * HEAD_DIM]

    def vt_slice(ref, g):
        return ref[0, 0, g * HEAD_DIM:(g + 1) * HEAD_DIM, :]

    sts = []
    for g in range(n_kv):
        qt = _heads_to_lanes(q_ref[0, :, g * qw:(g + 1) * qw])
        k = jnp.concatenate([head_slice(kp_ref, g), head_slice(kc_ref, g), head_slice(kn_ref, g)], axis=0)
        sts.append(jnp.dot(k, qt, preferred_element_type=F32))
    for g in range(n_kv):
        st = sts[g] * (log2e / math.sqrt(HEAD_DIM)) + bias
        sk = sink_ref[g] * log2e
        m = jnp.maximum(jnp.max(st, axis=0, keepdims=True), sk)
        p = jnp.exp2(st - m)
        denom = jnp.sum(p, axis=0, keepdims=True) + jnp.exp2(sk - m)
        vt = jnp.concatenate([vt_slice(vp_ref, g), vt_slice(vc_ref, g), vt_slice(vn_ref, g)], axis=1)
        ot = jnp.dot(vt, p.astype(BF16), preferred_element_type=F32) / denom
        o_ref[0, :, g * qw:(g + 1) * qw] = _lanes_to_heads(ot).astype(BF16)


def _window_bias():
    blk = WINDOW
    c = jnp.arange(3 * blk)[:, None]
    r = (jnp.arange(GROUP * blk) % blk)[None, :]
    band = jnp.abs(c - r - blk) <= WINDOW
    variants = []
    for last in (False, True):
        for first in (False, True):
            ok = band
            if first:
                ok = ok & (c >= blk)
            if last:
                ok = ok & (c < 2 * blk)
            variants.append(jnp.where(ok, 0.0, NEG).astype(F32))
    return jnp.stack([variants[0], variants[1], variants[2], variants[3]])


def _attn_window(qkv, vt, sink_rows, *, d_model):
    b, s, qkv_dim = qkv.shape
    n_kv = (qkv_dim - d_model) // (2 * HEAD_DIM)
    blk = WINDOW
    nb = s // blk
    sub = vt.shape[3] // blk
    kv_dim = n_kv * HEAD_DIM
    kcol = d_model // kv_dim
    bias = _window_bias()

    def k_spec(shift):
        return pl.BlockSpec((1, blk, kv_dim), lambda bi, i: (bi, jnp.clip(i + shift, 0, nb - 1), kcol))

    def vt_spec(shift):
        def index(bi, i):
            kb = jnp.clip(i + shift, 0, nb - 1)
            return (bi, kb // sub, 0, kb % sub)
        return pl.BlockSpec((1, 1, kv_dim, blk), index)

    def bias_index(bi, i):
        return ((i == 0).astype(jnp.int32) + 2 * (i == nb - 1).astype(jnp.int32), 0, 0)

    return pl.pallas_call(
        _attn_window_kernel,
        grid=(b, nb),
        in_specs=[pl.BlockSpec((1, blk, d_model), lambda bi, i: (bi, i, 0)),
                  k_spec(-1), k_spec(0), k_spec(1), vt_spec(-1), vt_spec(0), vt_spec(1),
                  pl.BlockSpec((n_kv, 1, GROUP * blk), lambda bi, i: (0, 0, 0)),
                  pl.BlockSpec((1, 3 * blk, GROUP * blk), bias_index)],
        out_specs=pl.BlockSpec((1, blk, d_model), lambda bi, i: (bi, i, 0)),
        out_shape=jax.ShapeDtypeStruct((b, s, d_model), BF16),
        compiler_params=_params("parallel", "parallel"),
        name="attn_window",
    )(qkv, qkv, qkv, qkv, vt, vt, vt, sink_rows, bias)


def _attn_dense_kernel(q_ref, k_ref, vt_ref, o_ref, qt_ref, acc0_ref, acc1_ref, acc2_ref, acc3_ref, *, unroll):
    tk = vt_ref.shape[3]
    qt_ref[...] = _heads_to_lanes(q_ref[0])
    for acc_ref in (acc0_ref, acc1_ref, acc2_ref, acc3_ref):
        acc_ref[...] = jnp.zeros(acc_ref.shape, F32)

    tq = qt_ref.shape[1] // GROUP
    accs = (acc0_ref, acc1_ref, acc2_ref, acc3_ref)
    ones = jnp.ones((ONES_ROWS, tk), BF16)

    def scores(c):
        off = pl.multiple_of(c * tk, tk)
        k = k_ref[0, pl.ds(off, tk), :]
        return [jnp.dot(k, qt_ref[:, h * tq:(h + 1) * tq], preferred_element_type=F32)
                for h in range(GROUP)]

    def update(c, h, st, m_prev):
        m_new = jnp.maximum(m_prev, jnp.max(st, axis=0, keepdims=True))
        p = jnp.exp2(st - m_new).astype(BF16)
        vt1 = jnp.concatenate([vt_ref[0, c], ones], axis=0)
        accs[h][...] = jnp.exp2(m_prev - m_new) * accs[h][...] + jnp.dot(vt1, p, preferred_element_type=F32)
        return m_new

    def body(cc, ms):
        ms = list(ms)
        c0 = cc * unroll
        st = scores(c0)
        for u in range(unroll):
            st_next = [None] * GROUP
            for h in range(GROUP):
                ms[h] = update(c0 + u, h, st[h], ms[h])
                if u + 1 < unroll and h == 0:
                    st_next = scores(c0 + u + 1)
            st = st_next
        return tuple(ms)

    init = tuple(jnp.full((1, tq), NEG, F32) for _ in range(GROUP))
    lax.fori_loop(0, vt_ref.shape[1] // unroll, body, init)
    ot = jnp.concatenate([accs[h][0:HEAD_DIM] / accs[h][HEAD_DIM:HEAD_DIM + 1] for h in range(GROUP)], axis=1)
    o_ref[0] = _lanes_to_heads(ot).astype(BF16)


def _attn_dense(qkv, vt, *, d_model):
    b, s, qkv_dim = qkv.shape
    n_kv = (qkv_dim - d_model) // (2 * HEAD_DIM)
    tq = _tile(s, 256)
    n_kt, tk = vt.shape[1], vt.shape[3]
    kcol = d_model // HEAD_DIM
    qw = GROUP * HEAD_DIM
    return pl.pallas_call(
        functools.partial(_attn_dense_kernel, unroll=_tile(n_kt, 8)),
        grid=(b, n_kv, s // tq),
        in_specs=[pl.BlockSpec((1, tq, qw), lambda bi, g, i: (bi, i, g)),
                  pl.BlockSpec((1, s, HEAD_DIM), lambda bi, g, i: (bi, 0, kcol + g)),
                  pl.BlockSpec((1, n_kt, HEAD_DIM, tk), lambda bi, g, i: (bi, 0, g, 0))],
        out_specs=pl.BlockSpec((1, tq, qw), lambda bi, g, i: (bi, i, g)),
        out_shape=jax.ShapeDtypeStruct((b, s, d_model), BF16),
        scratch_shapes=[pltpu.VMEM((HEAD_DIM, GROUP * tq), BF16)]
        + [pltpu.VMEM((HEAD_DIM + ONES_ROWS, tq), F32) for _ in range(GROUP)],
        compiler_params=_params("parallel", "parallel", "parallel"),
        name="attn_dense",
    )(qkv, qkv, vt)


def _oproj_kernel(o_ref, w_ref, x_ref, out_ref):
    out_ref[...] = x_ref[...] + jnp.dot(o_ref[...], w_ref[...], preferred_element_type=F32)


def _oproj(o2, w, x2):
    n, q_dim = o2.shape
    d = w.shape[1]
    tm = _tile(n, 512)
    tn = _tile(d, 1024)
    return pl.pallas_call(
        _oproj_kernel,
        grid=(n // tm, d // tn),
        in_specs=[pl.BlockSpec((tm, q_dim), lambda i, j: (i, 0)),
                  pl.BlockSpec((q_dim, tn), lambda i, j: (0, j)),
                  pl.BlockSpec((tm, tn), lambda i, j: (i, j))],
        out_specs=pl.BlockSpec((tm, tn), lambda i, j: (i, j)),
        out_shape=jax.ShapeDtypeStruct((n, d), F32),
        compiler_params=_params("parallel", "parallel"),
        name="o_proj",
    )(o2, w, x2)


def _norm_router_kernel(x_ref, g_ref, wr_ref, hn_ref, aff_ref):
    hn = _rms(x_ref[...], g_ref[...])
    hn_ref[...] = hn
    logits = lax.dot_general(wr_ref[...], hn.astype(BF16), _NT, preferred_element_type=F32)
    e = jnp.exp(logits - jnp.max(logits, axis=0, keepdims=True))
    aff_ref[...] = e / jnp.sum(e, axis=0, keepdims=True)


def _norm_router(x2, g, wr_t):
    n, d = x2.shape
    tm = _tile(n, 256)
    return pl.pallas_call(
        _norm_router_kernel,
        grid=(n // tm,),
        in_specs=[pl.BlockSpec((tm, d), lambda i: (i, 0)),
                  pl.BlockSpec((1, d), lambda i: (0, 0)),
                  pl.BlockSpec((N_EXPERTS, d), lambda i: (0, 0))],
        out_specs=[pl.BlockSpec((tm, d), lambda i: (i, 0)),
                   pl.BlockSpec((N_EXPERTS, tm), lambda i: (0, i))],
        out_shape=[jax.ShapeDtypeStruct((n, d), F32), jax.ShapeDtypeStruct((N_EXPERTS, n), F32)],
        compiler_params=_params("parallel"),
        name="norm_router",
    )(x2, g.reshape(1, d), wr_t)


def _final_norm_kernel(x_ref, g_ref, o_ref):
    o_ref[...] = _rms(x_ref[...], g_ref[...])


def _final_norm(x2, g):
    n, d = x2.shape
    tm = _tile(n, 256)
    return pl.pallas_call(
        _final_norm_kernel,
        grid=(n // tm,),
        in_specs=[pl.BlockSpec((tm, d), lambda i: (i, 0)), pl.BlockSpec((1, d), lambda i: (0, 0))],
        out_specs=pl.BlockSpec((tm, d), lambda i: (i, 0)),
        out_shape=jax.ShapeDtypeStruct((n, d), F32),
        compiler_params=_params("parallel"),
        name="final_norm",
    )(x2, g.reshape(1, d))


def _select_kernel(aff_ref, idx_ref, gate_ref, *, cap):
    a = aff_ref[0]
    nb = a.shape[0]
    bits = lax.bitcast_convert_type(a, jnp.int32)

    def count(mask):
        return jnp.sum(jnp.sum(jnp.where(mask, 1.0, 0.0), axis=1, keepdims=True), axis=0, keepdims=True)

    thr = jnp.zeros((1, 1), jnp.int32)
    for bit in range(30, -1, -1):
        cand = thr | (1 << bit)
        thr = jnp.where(count(bits >= cand) >= cap, cand, thr)

    tri_u = jnp.where(lax.broadcasted_iota(jnp.int32, (LANES, LANES), 0)
                      <= lax.broadcasted_iota(jnp.int32, (LANES, LANES), 1), 1.0, 0.0).astype(BF16)
    rk = lax.broadcasted_iota(jnp.int32, (nb, nb), 0)
    ck = lax.broadcasted_iota(jnp.int32, (nb, nb), 1)
    tri_strict_l = jnp.where(ck < rk, 1.0, 0.0).astype(BF16)
    tri_u_nb = jnp.where(rk <= ck, 1.0, 0.0).astype(BF16)

    def local_cumsum(maskf):
        return jnp.dot(maskf.astype(BF16), tri_u, preferred_element_type=F32)

    gt = bits > thr
    eq = bits == thr
    eqf = jnp.where(eq, 1.0, 0.0)
    need = cap - count(gt)
    eq_local = local_cumsum(eqf)
    eq_tot = jnp.broadcast_to(eq_local[:, LANES - 1:LANES], (nb, LANES))
    eq_before = jnp.dot(tri_strict_l, eq_tot.astype(BF16), preferred_element_type=F32)
    eq_rank = eq_before + eq_local - eqf
    sel = gt | (eq & (eq_rank < need))
    self_ = jnp.where(sel, 1.0, 0.0)

    cl = local_cumsum(self_)
    ones8 = jnp.ones((8, LANES), BF16)
    cnt_row = lax.dot_general(ones8, self_.astype(BF16), _NT, preferred_element_type=F32)
    bend_row = jnp.dot(cnt_row.astype(BF16), tri_u_nb, preferred_element_type=F32)[0:1]
    bprev_row = bend_row - cnt_row[0:1]

    p = lax.broadcasted_iota(jnp.int32, (cap, nb), 0).astype(F32)
    hot_blk = jnp.where((bprev_row <= p) & (p < bend_row), 1.0, 0.0)
    kcol = lax.broadcasted_iota(jnp.int32, (cap, nb), 1).astype(F32)
    blk_id = jnp.sum(hot_blk * kcol, axis=1, keepdims=True)
    p_local = p[:, 0:1] - jnp.sum(hot_blk * bprev_row, axis=1, keepdims=True)
    hot_b = hot_blk.astype(BF16)
    row_cl = jnp.dot(hot_b, cl.astype(BF16), preferred_element_type=F32)
    row_sel = jnp.dot(hot_b, self_.astype(BF16), preferred_element_type=F32)
    hot_tok = jnp.where((row_sel > 0.5) & (row_cl == p_local + 1.0), 1.0, 0.0)
    jcol = lax.broadcasted_iota(jnp.int32, (cap, LANES), 1).astype(F32)
    tok = blk_id * float(LANES) + jnp.sum(hot_tok * jcol, axis=1, keepdims=True)
    a1 = a.astype(BF16)
    r1 = a - a1.astype(F32)
    a2 = r1.astype(BF16)
    a3 = (r1 - a2.astype(F32)).astype(BF16)
    row_a = (jnp.dot(hot_b, a1, preferred_element_type=F32) + jnp.dot(hot_b, a2, preferred_element_type=F32)
             + jnp.dot(hot_b, a3, preferred_element_type=F32))
    idx_ref[0] = tok.astype(jnp.int32)
    gate_ref[0] = jnp.sum(hot_tok * row_a, axis=1, keepdims=True)


def _select(aff_t):
    e, n = aff_t.shape
    cap = CAPACITY_FACTOR * n // N_EXPERTS
    nb = n // LANES
    idx, gate = pl.pallas_call(
        functools.partial(_select_kernel, cap=cap),
        grid=(e,),
        in_specs=[pl.BlockSpec((1, nb, LANES), lambda i: (i, 0, 0))],
        out_specs=[pl.BlockSpec((1, cap, 1), lambda i: (i, 0, 0)),
                   pl.BlockSpec((1, cap, 1), lambda i: (i, 0, 0))],
        out_shape=[jax.ShapeDtypeStruct((e, cap, 1), jnp.int32), jax.ShapeDtypeStruct((e, cap, 1), F32)],
        compiler_params=_params("parallel"),
        name="expert_select",
    )(aff_t.reshape(e, nb, LANES))
    return idx.reshape(e * cap), gate.reshape(e * cap, 1)


def _moe_kernel(idx_ref, hn_hbm, gate_ref, wg_ref, wu_ref, wd_ref, x_in_hbm, x_hbm,
                rows, xe_ref, acc_ref, hn_sem, xin_sem, xout_sem, *, tm, tiles_per_expert, n_tiles, dn):
    del x_in_hbm
    f = pl.program_id(2)
    t = pl.program_id(0) * tiles_per_expert + pl.program_id(1)
    slot = t % 2
    other = 1 - slot
    half = tm // 2

    def hn_copy(tile, slot_):
        return lambda row: pltpu.make_async_copy(
            hn_hbm.at[pl.ds(idx_ref[tile * tm + row], 1)], rows.at[slot_, pl.ds(row, 1)], hn_sem.at[slot_])

    def x_in_copy(row):
        return pltpu.make_async_copy(
            x_hbm.at[pl.ds(idx_ref[t * tm + row], 1)], rows.at[slot, pl.ds(row, 1)], xin_sem)

    def x_out_copy(tile, slot_):
        return lambda row: pltpu.make_async_copy(
            rows.at[slot_, pl.ds(row, 1)], x_hbm.at[pl.ds(idx_ref[tile * tm + row], 1)], xout_sem)

    def start_all(make):
        def body(r, carry):
            make(r).start()
            return carry
        lax.fori_loop(0, tm, body, 0, unroll=8)

    def start_half(make, part):
        for r in range(half):
            make(part * half + r).start()

    def wait_hn():
        pltpu.make_async_copy(hn_hbm.at[pl.ds(0, tm)], rows.at[slot], hn_sem.at[slot]).wait()

    def wait_x_in():
        pltpu.make_async_copy(x_hbm.at[pl.ds(0, tm)], rows.at[slot], xin_sem).wait()

    def wait_x_out():
        pltpu.make_async_copy(rows.at[slot], x_hbm.at[pl.ds(0, tm)], xout_sem).wait()

    @pl.when(f == 0)
    def _():
        @pl.when(t == 0)
        def _():
            start_all(hn_copy(0, 0))
        wait_hn()
        xe_ref[...] = rows[slot].astype(BF16)
        acc_ref[...] = jnp.zeros(acc_ref.shape, F32)

    @pl.when((f == 3) & (t > 0))
    def _():
        wait_x_out()

    def step(issue=None):
        if issue is not None:
            issue()
        xe = xe_ref[...]
        g = jnp.dot(xe, wg_ref[0], preferred_element_type=F32)
        u = jnp.dot(xe, wu_ref[0], preferred_element_type=F32)
        h = (g * jax.nn.sigmoid(g) * u).astype(BF16)
        for c in range(acc_ref.shape[1] // dn):
            cols = slice(c * dn, (c + 1) * dn)
            acc_ref[:, cols] += jnp.dot(h, wd_ref[0, :, cols], preferred_element_type=F32)

    has_prev = t > 0
    has_next = t + 1 < n_tiles
    variants = [
        ((f == 0) & has_prev, lambda: start_half(x_out_copy(t - 1, other), 0)),
        ((f == 1) & has_prev, lambda: start_half(x_out_copy(t - 1, other), 1)),
        (f == 3, lambda: start_half(x_in_copy, 0)),
        (f == 4, lambda: start_half(x_in_copy, 1)),
        ((f == 5) & has_next, lambda: start_half(hn_copy(t + 1, other), 0)),
        ((f == 6) & has_next, lambda: start_half(hn_copy(t + 1, other), 1)),
    ]
    issuing = variants[0][0]
    for cond, issue in variants:
        pl.when(cond)(functools.partial(step, issue))
        issuing = issuing | cond
    pl.when(jnp.logical_not(issuing))(step)

    @pl.when(f == MOE_STEPS - 1)
    def _():
        wait_x_in()
        rows[slot] = rows[slot] + acc_ref[...] * gate_ref[...]

        @pl.when(t == n_tiles - 1)
        def _():
            start_all(x_out_copy(t, slot))
            wait_x_out()


def _moe(x2, hn, idx, gate, wg, wu, wd, layer):
    n, d = x2.shape
    e = N_EXPERTS
    d_exp = wg.shape[2]
    e0 = layer * e
    cap = idx.shape[0] // e
    tm = _tile(cap, 512)
    assert d_exp % MOE_STEPS == 0 and tm % 2 == 0
    tf = d_exp // MOE_STEPS
    tiles_per_expert = cap // tm
    kern = functools.partial(_moe_kernel, tm=tm, tiles_per_expert=tiles_per_expert,
                             n_tiles=e * tiles_per_expert, dn=_tile(d, 1024))
    grid_spec = pltpu.PrefetchScalarGridSpec(
        num_scalar_prefetch=1,
        grid=(e, tiles_per_expert, d_exp // tf),
        in_specs=[
            pl.BlockSpec(memory_space=pl.ANY),
            pl.BlockSpec((tm, 1), lambda ei, i, f, idx_: (ei * tiles_per_expert + i, 0)),
            pl.BlockSpec((1, d, tf), lambda ei, i, f, idx_: (e0 + ei, 0, f)),
            pl.BlockSpec((1, d, tf), lambda ei, i, f, idx_: (e0 + ei, 0, f)),
            pl.BlockSpec((1, tf, d), lambda ei, i, f, idx_: (e0 + ei, f, 0)),
            pl.BlockSpec(memory_space=pl.ANY),
        ],
        out_specs=pl.BlockSpec(memory_space=pl.ANY),
        scratch_shapes=[
            pltpu.VMEM((2, tm, d), F32),
            pltpu.VMEM((tm, d), BF16),
            pltpu.VMEM((tm, d), F32),
            pltpu.SemaphoreType.DMA((2,)),
            pltpu.SemaphoreType.DMA(()),
            pltpu.SemaphoreType.DMA(()),
        ],
    )
    return pl.pallas_call(
        kern,
        grid_spec=grid_spec,
        out_shape=jax.ShapeDtypeStruct((n, d), F32),
        input_output_aliases={6: 0},
        compiler_params=_params("arbitrary", "arbitrary", "arbitrary"),
        name="moe_experts",
    )(idx, hn, gate, wg, wu, wd, x2)


def _angles(pos, dim):
    inv_freq = ROPE_THETA ** (-jnp.arange(0, dim, 2, dtype=F32) / dim)
    return pos.astype(F32)[:, None] * inv_freq[None, :]


def _rope_tables_1d(s):
    ang = _angles(jnp.arange(s), HEAD_DIM)
    cos, sin = jnp.cos(ang), jnp.sin(ang)
    return jnp.concatenate([cos, cos], axis=1), jnp.concatenate([-sin, sin], axis=1)


def _rope_tables_axial(s):
    t = jnp.arange(s)
    half = HEAD_DIM // 2
    ar, ac = _angles(t // GRID_W, half), _angles(t % GRID_W, half)
    cos = jnp.concatenate([jnp.cos(ar), jnp.cos(ar), jnp.cos(ac), jnp.cos(ac)], axis=1)
    sin = jnp.concatenate([-jnp.sin(ar), jnp.sin(ar), -jnp.sin(ac), jnp.sin(ac)], axis=1)
    return cos, sin


def _trunk(x, norm_mix, norm_ffn, final_norm, a_w_qkv, a_w_o, a_sink, b_w_qkv, b_w_o, b_q_norm, b_k_norm,
           wr_t, w_gate, w_up, w_down):
    b, s, d = x.shape
    n = b * s
    depth = norm_mix.shape[0]
    qkv_dim = a_w_qkv.shape[2]
    def by_kind(cos, sin, q_scale):
        return (jnp.stack([cos * q_scale, cos, jnp.ones_like(cos)]),
                jnp.stack([sin * q_scale, sin, jnp.zeros_like(sin)]))

    cos_a, sin_a = by_kind(*_rope_tables_1d(s), 1.0)
    cos_b, sin_b = by_kind(*_rope_tables_axial(s), DENSE_Q_SCALE)
    ones = jnp.ones((HEAD_DIM,), F32)
    gain_a = jnp.stack([ones, ones, ones]).reshape(3, 1, HEAD_DIM)
    x2 = x.reshape(n, d)
    for i in range(depth):
        j = i // N_MIXERS
        if i % N_MIXERS == 0:
            qkv, vt = _qkv(x2, norm_mix[i], a_w_qkv[j], cos_a, sin_a, gain_a,
                           seq_len=s, qk_norm=False, axial=False)
            vt = vt.reshape(b, -1, vt.shape[1], vt.shape[2])
            sink_rows = jnp.repeat(a_sink[j].astype(F32), WINDOW).reshape(-1, 1, GROUP * WINDOW)
            o = _attn_window(qkv.reshape(b, s, qkv_dim), vt, sink_rows, d_model=d)
            x2 = _oproj(o.reshape(n, d), a_w_o[j], x2)
        else:
            gain_b = jnp.stack([b_q_norm[j].astype(F32), b_k_norm[j].astype(F32), ones]).reshape(3, 1, HEAD_DIM)
            qkv, vt = _qkv(x2, norm_mix[i], b_w_qkv[j], cos_b, sin_b, gain_b,
                           seq_len=s, qk_norm=True, axial=True)
            vt = vt.reshape(b, -1, vt.shape[1], vt.shape[2])
            o = _attn_dense(qkv.reshape(b, s, qkv_dim), vt, d_model=d)
            x2 = _oproj(o.reshape(n, d), b_w_o[j], x2)
        hn, aff_t = _norm_router(x2, norm_ffn[i], wr_t[i])
        idx, gate = _select(aff_t)
        x2 = _moe(x2, hn, idx, gate, w_gate, w_up, w_down, i)
    return _final_norm(x2, final_norm).reshape(b, s, d)


def kernel(x_prompt, x_sample, norm_mix, norm_ffn, final_norm, a_w_qkv, a_w_o, a_sink, b_w_qkv, b_w_o,
           b_q_norm, b_k_norm, w_router, w_gate, w_up, w_down):
    def experts(w):
        return w.astype(BF16).reshape((-1,) + w.shape[2:])

    weights = (a_w_qkv.astype(BF16), a_w_o.astype(BF16), a_sink, b_w_qkv.astype(BF16), b_w_o.astype(BF16),
               b_q_norm, b_k_norm, jnp.swapaxes(w_router, 1, 2).astype(BF16),
               experts(w_gate), experts(w_up), experts(w_down))
    y_prompt = _trunk(x_prompt, norm_mix, norm_ffn, final_norm, *weights)
    y_sample = _trunk(x_sample, norm_mix, norm_ffn, final_norm, *weights)
    return (y_prompt, y_sample)
```

```python
import functools
import math

import jax
import jax.numpy as jnp
from jax import lax
from jax.experimental import pallas as pl
from jax.experimental.pallas import tpu as pltpu

HEAD_DIM = 128
GROUP = 4
WINDOW = 128
ROPE_THETA = 10000.0
GRID_W = 64
N_EXPERTS = 16
CAPACITY_FACTOR = 2
N_MIXERS = 2
EPS = 1e-6
NEG = -1e30
LANES = 128
SEQ_TILE = 512
MOE_STEPS = 8
ONES_ROWS = 16
DENSE_Q_SCALE = math.log2(math.e) / math.sqrt(HEAD_DIM)
V7X_VMEM_LIMIT_BYTES = 56 * 1024 * 1024

BF16 = jnp.bfloat16
F32 = jnp.float32
_NT = (((1,), (1,)), ((), ()))


def _params(*sem):
    return pltpu.CompilerParams(dimension_semantics=sem, vmem_limit_bytes=V7X_VMEM_LIMIT_BYTES)


def _tile(n, target):
    t = min(n, target)
    while n % t:
        t -= 1
    return t


def _rms(x, g):
    return x * lax.rsqrt(jnp.mean(x * x, axis=-1, keepdims=True) + EPS) * g


def _qkv_kernel(x_ref, g_ref, w_ref, cos_ref, sin_ref, gain_ref, o_ref, vt_ref, hn_ref, *,
                n_qk_tiles, qk_norm, axial, cols):
    j = pl.program_id(1)
    is_v = j >= n_qk_tiles

    @pl.when(j == 0)
    def _():
        hn_ref[...] = _rms(x_ref[...], g_ref[...]).astype(BF16)

    tm, tn = o_ref.shape
    cos = cos_ref[0]
    sin = sin_ref[0]
    gain = gain_ref[0]
    if axial:
        lane = lax.broadcasted_iota(jnp.int32, (tm, HEAD_DIM), 1)
        low_half = (lane % (HEAD_DIM // 2)) < (HEAD_DIM // 4)
    outs = []
    for cb in range(tn // cols):
        acc = jnp.dot(hn_ref[...], w_ref[:, cb * cols:(cb + 1) * cols], preferred_element_type=F32)
        heads = []
        for h in range(cols // HEAD_DIM):
            a = acc[:, h * HEAD_DIM:(h + 1) * HEAD_DIM]
            if qk_norm:
                a = jnp.where(is_v, a, _rms(a, gain))
            if axial:
                partner = jnp.where(low_half, pltpu.roll(a, 3 * HEAD_DIM // 4, 1),
                                    pltpu.roll(a, HEAD_DIM // 4, 1))
            else:
                partner = pltpu.roll(a, HEAD_DIM // 2, 1)
            heads.append(a * cos + partner * sin)
        out = jnp.concatenate(heads, axis=1)
        o_ref[:, cb * cols:(cb + 1) * cols] = out.astype(BF16)
        outs.append(out)

    @pl.when(is_v)
    def _():
        vt_ref[0] = jnp.concatenate(outs, axis=1).T.astype(BF16)


def _qkv(x2, g, w, cos3, sin3, gain3, *, seq_len, qk_norm, axial):
    n, d = x2.shape
    qkv_dim = w.shape[1]
    kv_dim = (qkv_dim - d) // 2
    tm = _tile(seq_len, SEQ_TILE)
    tn = _tile(kv_dim, 1024)
    s_tiles = seq_len // tm
    n_q_tiles = d // tn
    n_qk_tiles = (d + kv_dim) // tn
    kern = functools.partial(_qkv_kernel, n_qk_tiles=n_qk_tiles, qk_norm=qk_norm, axial=axial,
                             cols=_tile(tn, 2 * HEAD_DIM))

    def kind(j):
        return (j >= n_q_tiles).astype(jnp.int32) + (j >= n_qk_tiles).astype(jnp.int32)

    return pl.pallas_call(
        kern,
        grid=(n // tm, qkv_dim // tn),
        in_specs=[
            pl.BlockSpec((tm, d), lambda i, j: (i, 0)),
            pl.BlockSpec((1, d), lambda i, j: (0, 0)),
            pl.BlockSpec((d, tn), lambda i, j: (0, j)),
            pl.BlockSpec((1, tm, HEAD_DIM), lambda i, j: (kind(j), i % s_tiles, 0)),
            pl.BlockSpec((1, tm, HEAD_DIM), lambda i, j: (kind(j), i % s_tiles, 0)),
            pl.BlockSpec((1, 1, HEAD_DIM), lambda i, j: (kind(j), 0, 0)),
        ],
        out_specs=[pl.BlockSpec((tm, tn), lambda i, j: (i, j)),
                   pl.BlockSpec((1, tn, tm), lambda i, j: (i, jnp.maximum(j - n_qk_tiles, 0), 0))],
        out_shape=[jax.ShapeDtypeStruct((n, qkv_dim), BF16),
                   jax.ShapeDtypeStruct((n // tm, kv_dim, tm), BF16)],
        scratch_shapes=[pltpu.VMEM((tm, d), BF16)],
        compiler_params=_params("parallel", "arbitrary"),
        name="qkv_proj",
    )(x2, g.reshape(1, d), w, cos3, sin3, gain3)


def _heads_to_lanes(q):
    qf = q.astype(F32)
    return jnp.concatenate([qf[:, h * HEAD_DIM:(h + 1) * HEAD_DIM].T for h in range(GROUP)],
                           axis=1).astype(BF16)


def _lanes_to_heads(ot):
    t = ot.shape[1] // GROUP
    return jnp.concatenate([ot[:, h * t:(h + 1) * t].T for h in range(GROUP)], axis=1)


def _attn_window_kernel(q_ref, kp_ref, kc_ref, kn_ref, vp_ref, vc_ref, vn_ref, sink_ref, bias_ref, o_ref):
    log2e = math.log2(math.e)
    qw = GROUP * HEAD_DIM
    n_kv = q_ref.shape[2] // qw
    bias = bias_ref[0]

    def head_slice(ref, g):
        return ref[0, :, g * HEAD_DIM:(g + 1) * HEAD_DIM]

    def vt_slice(ref, g):
        return ref[0, 0, g * HEAD_DIM:(g + 1) * HEAD_DIM, :]

    sts = []
    for g in range(n_kv):
        qt = _heads_to_lanes(q_ref[0, :, g * qw:(g + 1) * qw])
        k = jnp.concatenate([head_slice(kp_ref, g), head_slice(kc_ref, g), head_slice(kn_ref, g)], axis=0)
        sts.append(jnp.dot(k, qt, preferred_element_type=F32))
    for g in range(n_kv):
        st = sts[g] * (log2e / math.sqrt(HEAD_DIM)) + bias
        sk = sink_ref[g] * log2e
        m = jnp.maximum(jnp.max(st, axis=0, keepdims=True), sk)
        p = jnp.exp2(st - m)
        denom = jnp.sum(p, axis=0, keepdims=True) + jnp.exp2(sk - m)
        vt = jnp.concatenate([vt_slice(vp_ref, g), vt_slice(vc_ref, g), vt_slice(vn_ref, g)], axis=1)
        ot = jnp.dot(vt, p.astype(BF16), preferred_element_type=F32) / denom
        o_ref[0, :, g * qw:(g + 1) * qw] = _lanes_to_heads(ot).astype(BF16)


def _window_bias():
    blk = WINDOW
    c = jnp.arange(3 * blk)[:, None]
    r = (jnp.arange(GROUP * blk) % blk)[None, :]
    band = jnp.abs(c - r - blk) <= WINDOW
    variants = []
    for last in (False, True):
        for first in (False, True):
            ok = band
            if first:
                ok = ok & (c >= blk)
            if last:
                ok = ok & (c < 2 * blk)
            variants.append(jnp.where(ok, 0.0, NEG).astype(F32))
    return jnp.stack([variants[0], variants[1], variants[2], variants[3]])


def _attn_window(qkv, vt, sink_rows, *, d_model):
    b, s, qkv_dim = qkv.shape
    n_kv = (qkv_dim - d_model) // (2 * HEAD_DIM)
    blk = WINDOW
    nb = s // blk
    sub = vt.shape[3] // blk
    kv_dim = n_kv * HEAD_DIM
    kcol = d_model // kv_dim
    bias = _window_bias()

    def k_spec(shift):
        return pl.BlockSpec((1, blk, kv_dim), lambda bi, i: (bi, jnp.clip(i + shift, 0, nb - 1), kcol))

    def vt_spec(shift):
        def index(bi, i):
            kb = jnp.clip(i + shift, 0, nb - 1)
            return (bi, kb // sub, 0, kb % sub)
        return pl.BlockSpec((1, 1, kv_dim, blk), index)

    def bias_index(bi, i):
        return ((i == 0).astype(jnp.int32) + 2 * (i == nb - 1).astype(jnp.int32), 0, 0)

    return pl.pallas_call(
        _attn_window_kernel,
        grid=(b, nb),
        in_specs=[pl.BlockSpec((1, blk, d_model), lambda bi, i: (bi, i, 0)),
                  k_spec(-1), k_spec(0), k_spec(1), vt_spec(-1), vt_spec(0), vt_spec(1),
                  pl.BlockSpec((n_kv, 1, GROUP * blk), lambda bi, i: (0, 0, 0)),
                  pl.BlockSpec((1, 3 * blk, GROUP * blk), bias_index)],
        out_specs=pl.BlockSpec((1, blk, d_model), lambda bi, i: (bi, i, 0)),
        out_shape=jax.ShapeDtypeStruct((b, s, d_model), BF16),
        compiler_params=_params("parallel", "parallel"),
        name="attn_window",
    )(qkv, qkv, qkv, qkv, vt, vt, vt, sink_rows, bias)


def _attn_dense_kernel(q_ref, k_ref, vt_ref, o_ref, qt_ref, acc0_ref, acc1_ref, acc2_ref, acc3_ref, *, unroll):
    tk = vt_ref.shape[3]
    qt_ref[...] = _heads_to_lanes(q_ref[0])
    for acc_ref in (acc0_ref, acc1_ref, acc2_ref, acc3_ref):
        acc_ref[...] = jnp.zeros(acc_ref.shape, F32)

    tq = qt_ref.shape[1] // GROUP
    accs = (acc0_ref, acc1_ref, acc2_ref, acc3_ref)
    ones = jnp.ones((ONES_ROWS, tk), BF16)

    def scores(c):
        off = pl.multiple_of(c * tk, tk)
        k = k_ref[0, pl.ds(off, tk), :]
        return [jnp.dot(k, qt_ref[:, h * tq:(h + 1) * tq], preferred_element_type=F32)
                for h in range(GROUP)]

    def update(c, h, st, m_prev):
        m_new = jnp.maximum(m_prev, jnp.max(st, axis=0, keepdims=True))
        p = jnp.exp2(st - m_new).astype(BF16)
        vt1 = jnp.concatenate([vt_ref[0, c], ones], axis=0)
        accs[h][...] = jnp.exp2(m_prev - m_new) * accs[h][...] + jnp.dot(vt1, p, preferred_element_type=F32)
        return m_new

    def body(cc, ms):
        ms = list(ms)
        c0 = cc * unroll
        st = scores(c0)
        for u in range(unroll):
            st_next = [None] * GROUP
            for h in range(GROUP):
                ms[h] = update(c0 + u, h, st[h], ms[h])
                if u + 1 < unroll and h == 0:
                    st_next = scores(c0 + u + 1)
            st = st_next
        return tuple(ms)

    init = tuple(jnp.full((1, tq), NEG, F32) for _ in range(GROUP))
    lax.fori_loop(0, vt_ref.shape[1] // unroll, body, init)
    ot = jnp.concatenate([accs[h][0:HEAD_DIM] / accs[h][HEAD_DIM:HEAD_DIM + 1] for h in range(GROUP)], axis=1)
    o_ref[0] = _lanes_to_heads(ot).astype(BF16)


def _attn_dense(qkv, vt, *, d_model):
    b, s, qkv_dim = qkv.shape
    n_kv = (qkv_dim - d_model) // (2 * HEAD_DIM)
    tq = _tile(s, 256)
    n_kt, tk = vt.shape[1], vt.shape[3]
    kcol = d_model // HEAD_DIM
    qw = GROUP * HEAD_DIM
    return pl.pallas_call(
        functools.partial(_attn_dense_kernel, unroll=_tile(n_kt, 16)),
        grid=(b, n_kv, s // tq),
        in_specs=[pl.BlockSpec((1, tq, qw), lambda bi, g, i: (bi, i, g)),
                  pl.BlockSpec((1, s, HEAD_DIM), lambda bi, g, i: (bi, 0, kcol + g)),
                  pl.BlockSpec((1, n_kt, HEAD_DIM, tk), lambda bi, g, i: (bi, 0, g, 0))],
        out_specs=pl.BlockSpec((1, tq, qw), lambda bi, g, i: (bi, i, g)),
        out_shape=jax.ShapeDtypeStruct((b, s, d_model), BF16),
        scratch_shapes=[pltpu.VMEM((HEAD_DIM, GROUP * tq), BF16)]
        + [pltpu.VMEM((HEAD_DIM + ONES_ROWS, tq), F32) for _ in range(GROUP)],
        compiler_params=_params("parallel", "parallel", "parallel"),
        name="attn_dense",
    )(qkv, qkv, vt)


def _oproj_kernel(o_ref, w_ref, x_ref, out_ref):
    out_ref[...] = x_ref[...] + jnp.dot(o_ref[...], w_ref[...], preferred_element_type=F32)


def _oproj(o2, w, x2):
    n, q_dim = o2.shape
    d = w.shape[1]
    tm = _tile(n, 512)
    tn = _tile(d, 1024)
    return pl.pallas_call(
        _oproj_kernel,
        grid=(n // tm, d // tn),
        in_specs=[pl.BlockSpec((tm, q_dim), lambda i, j: (i, 0)),
                  pl.BlockSpec((q_dim, tn), lambda i, j: (0, j)),
                  pl.BlockSpec((tm, tn), lambda i, j: (i, j))],
        out_specs=pl.BlockSpec((tm, tn), lambda i, j: (i, j)),
        out_shape=jax.ShapeDtypeStruct((n, d), F32),
        compiler_params=_params("parallel", "parallel"),
        name="o_proj",
    )(o2, w, x2)


def _norm_router_kernel(x_ref, g_ref, wr_ref, hn_ref, aff_ref):
    hn = _rms(x_ref[...], g_ref[...])
    hn_ref[...] = hn
    logits = lax.dot_general(wr_ref[...], hn.astype(BF16), _NT, preferred_element_type=F32)
    e = jnp.exp(logits - jnp.max(logits, axis=0, keepdims=True))
    aff_ref[...] = e / jnp.sum(e, axis=0, keepdims=True)


def _norm_router(x2, g, wr_t):
    n, d = x2.shape
    tm = _tile(n, 256)
    return pl.pallas_call(
        _norm_router_kernel,
        grid=(n // tm,),
        in_specs=[pl.BlockSpec((tm, d), lambda i: (i, 0)),
                  pl.BlockSpec((1, d), lambda i: (0, 0)),
                  pl.BlockSpec((N_EXPERTS, d), lambda i: (0, 0))],
        out_specs=[pl.BlockSpec((tm, d), lambda i: (i, 0)),
                   pl.BlockSpec((N_EXPERTS, tm), lambda i: (0, i))],
        out_shape=[jax.ShapeDtypeStruct((n, d), F32), jax.ShapeDtypeStruct((N_EXPERTS, n), F32)],
        compiler_params=_params("parallel"),
        name="norm_router",
    )(x2, g.reshape(1, d), wr_t)


def _final_norm_kernel(x_ref, g_ref, o_ref):
    o_ref[...] = _rms(x_ref[...], g_ref[...])


def _final_norm(x2, g):
    n, d = x2.shape
    tm = _tile(n, 256)
    return pl.pallas_call(
        _final_norm_kernel,
        grid=(n // tm,),
        in_specs=[pl.BlockSpec((tm, d), lambda i: (i, 0)), pl.BlockSpec((1, d), lambda i: (0, 0))],
        out_specs=pl.BlockSpec((tm, d), lambda i: (i, 0)),
        out_shape=jax.ShapeDtypeStruct((n, d), F32),
        compiler_params=_params("parallel"),
        name="final_norm",
    )(x2, g.reshape(1, d))


def _select_kernel(aff_ref, idx_ref, gate_ref, *, cap):
    a = aff_ref[0]
    nb = a.shape[0]
    bits = lax.bitcast_convert_type(a, jnp.int32)

    def count(mask):
        return jnp.sum(jnp.sum(jnp.where(mask, 1.0, 0.0), axis=1, keepdims=True), axis=0, keepdims=True)

    thr = jnp.zeros((1, 1), jnp.int32)
    for bit in range(30, -1, -1):
        cand = thr | (1 << bit)
        thr = jnp.where(count(bits >= cand) >= cap, cand, thr)

    tri_u = jnp.where(lax.broadcasted_iota(jnp.int32, (LANES, LANES), 0)
                      <= lax.broadcasted_iota(jnp.int32, (LANES, LANES), 1), 1.0, 0.0).astype(BF16)
    rk = lax.broadcasted_iota(jnp.int32, (nb, nb), 0)
    ck = lax.broadcasted_iota(jnp.int32, (nb, nb), 1)
    tri_strict_l = jnp.where(ck < rk, 1.0, 0.0).astype(BF16)
    tri_u_nb = jnp.where(rk <= ck, 1.0, 0.0).astype(BF16)

    def local_cumsum(maskf):
        return jnp.dot(maskf.astype(BF16), tri_u, preferred_element_type=F32)

    gt = bits > thr
    eq = bits == thr
    eqf = jnp.where(eq, 1.0, 0.0)
    need = cap - count(gt)
    eq_local = local_cumsum(eqf)
    eq_tot = jnp.broadcast_to(eq_local[:, LANES - 1:LANES], (nb, LANES))
    eq_before = jnp.dot(tri_strict_l, eq_tot.astype(BF16), preferred_element_type=F32)
    eq_rank = eq_before + eq_local - eqf
    sel = gt | (eq & (eq_rank < need))
    self_ = jnp.where(sel, 1.0, 0.0)

    cl = local_cumsum(self_)
    ones8 = jnp.ones((8, LANES), BF16)
    cnt_row = lax.dot_general(ones8, self_.astype(BF16), _NT, preferred_element_type=F32)
    bend_row = jnp.dot(cnt_row.astype(BF16), tri_u_nb, preferred_element_type=F32)[0:1]
    bprev_row = bend_row - cnt_row[0:1]

    p = lax.broadcasted_iota(jnp.int32, (cap, nb), 0).astype(F32)
    hot_blk = jnp.where((bprev_row <= p) & (p < bend_row), 1.0, 0.0)
    kcol = lax.broadcasted_iota(jnp.int32, (cap, nb), 1).astype(F32)
    blk_id = jnp.sum(hot_blk * kcol, axis=1, keepdims=True)
    p_local = p[:, 0:1] - jnp.sum(hot_blk * bprev_row, axis=1, keepdims=True)
    hot_b = hot_blk.astype(BF16)
    row_cl = jnp.dot(hot_b, cl.astype(BF16), preferred_element_type=F32)
    row_sel = jnp.dot(hot_b, self_.astype(BF16), preferred_element_type=F32)
    hot_tok = jnp.where((row_sel > 0.5) & (row_cl == p_local + 1.0), 1.0, 0.0)
    jcol = lax.broadcasted_iota(jnp.int32, (cap, LANES), 1).astype(F32)
    tok = blk_id * float(LANES) + jnp.sum(hot_tok * jcol, axis=1, keepdims=True)
    a1 = a.astype(BF16)
    r1 = a - a1.astype(F32)
    a2 = r1.astype(BF16)
    a3 = (r1 - a2.astype(F32)).astype(BF16)
    row_a = (jnp.dot(hot_b, a1, preferred_element_type=F32) + jnp.dot(hot_b, a2, preferred_element_type=F32)
             + jnp.dot(hot_b, a3, preferred_element_type=F32))
    idx_ref[0] = tok.astype(jnp.int32)
    gate_ref[0] = jnp.sum(hot_tok * row_a, axis=1, keepdims=True)


def _select(aff_t):
    e, n = aff_t.shape
    cap = CAPACITY_FACTOR * n // N_EXPERTS
    nb = n // LANES
    idx, gate = pl.pallas_call(
        functools.partial(_select_kernel, cap=cap),
        grid=(e,),
        in_specs=[pl.BlockSpec((1, nb, LANES), lambda i: (i, 0, 0))],
        out_specs=[pl.BlockSpec((1, cap, 1), lambda i: (i, 0, 0)),
                   pl.BlockSpec((1, cap, 1), lambda i: (i, 0, 0))],
        out_shape=[jax.ShapeDtypeStruct((e, cap, 1), jnp.int32), jax.ShapeDtypeStruct((e, cap, 1), F32)],
        compiler_params=_params("parallel"),
        name="expert_select",
    )(aff_t.reshape(e, nb, LANES))
    return idx.reshape(e * cap), gate.reshape(e * cap, 1)


def _moe_kernel(idx_ref, hn_hbm, gate_ref, wg_ref, wu_ref, wd_ref, x_in_hbm, x_hbm,
                rows, xe_ref, acc_ref, hn_sem, xin_sem, xout_sem, *, tm, tiles_per_expert, n_tiles, dn):
    del x_in_hbm
    f = pl.program_id(2)
    t = pl.program_id(0) * tiles_per_expert + pl.program_id(1)
    slot = t % 2
    other = 1 - slot
    half = tm // 2

    def hn_copy(tile, slot_):
        return lambda row: pltpu.make_async_copy(
            hn_hbm.at[pl.ds(idx_ref[tile * tm + row], 1)], rows.at[slot_, pl.ds(row, 1)], hn_sem.at[slot_])

    def x_in_copy(row):
        return pltpu.make_async_copy(
            x_hbm.at[pl.ds(idx_ref[t * tm + row], 1)], rows.at[slot, pl.ds(row, 1)], xin_sem)

    def x_out_copy(tile, slot_):
        return lambda row: pltpu.make_async_copy(
            rows.at[slot_, pl.ds(row, 1)], x_hbm.at[pl.ds(idx_ref[tile * tm + row], 1)], xout_sem)

    def start_all(make):
        def body(r, carry):
            make(r).start()
            return carry
        lax.fori_loop(0, tm, body, 0, unroll=8)

    def start_half(make, part):
        for r in range(half):
            make(part * half + r).start()

    def wait_hn():
        pltpu.make_async_copy(hn_hbm.at[pl.ds(0, tm)], rows.at[slot], hn_sem.at[slot]).wait()

    def wait_x_in():
        pltpu.make_async_copy(x_hbm.at[pl.ds(0, tm)], rows.at[slot], xin_sem).wait()

    def wait_x_out():
        pltpu.make_async_copy(rows.at[slot], x_hbm.at[pl.ds(0, tm)], xout_sem).wait()

    @pl.when(f == 0)
    def _():
        @pl.when(t == 0)
        def _():
            start_all(hn_copy(0, 0))
        wait_hn()
        xe_ref[...] = rows[slot].astype(BF16)
        acc_ref[...] = jnp.zeros(acc_ref.shape, F32)

    @pl.when((f == 3) & (t > 0))
    def _():
        wait_x_out()

    def step(issue=None):
        if issue is not None:
            issue()
        xe = xe_ref[...]
        g = jnp.dot(xe, wg_ref[0], preferred_element_type=F32)
        u = jnp.dot(xe, wu_ref[0], preferred_element_type=F32)
        h = (g * jax.nn.sigmoid(g) * u).astype(BF16)
        for c in range(acc_ref.shape[1] // dn):
            cols = slice(c * dn, (c + 1) * dn)
            acc_ref[:, cols] += jnp.dot(h, wd_ref[0, :, cols], preferred_element_type=F32)

    has_prev = t > 0
    has_next = t + 1 < n_tiles
    variants = [
        ((f == 0) & has_prev, lambda: start_half(x_out_copy(t - 1, other), 0)),
        ((f == 1) & has_prev, lambda: start_half(x_out_copy(t - 1, other), 1)),
        (f == 3, lambda: start_half(x_in_copy, 0)),
        (f == 4, lambda: start_half(x_in_copy, 1)),
        ((f == 5) & has_next, lambda: start_half(hn_copy(t + 1, other), 0)),
        ((f == 6) & has_next, lambda: start_half(hn_copy(t + 1, other), 1)),
    ]
    issuing = variants[0][0]
    for cond, issue in variants:
        pl.when(cond)(functools.partial(step, issue))
        issuing = issuing | cond
    pl.when(jnp.logical_not(issuing))(step)

    @pl.when(f == MOE_STEPS - 1)
    def _():
        wait_x_in()
        rows[slot] = rows[slot] + acc_ref[...] * gate_ref[...]

        @pl.when(t == n_tiles - 1)
        def _():
            start_all(x_out_copy(t, slot))
            wait_x_out()


def _moe(x2, hn, idx, gate, wg, wu, wd, layer):
    n, d = x2.shape
    e = N_EXPERTS
    d_exp = wg.shape[2]
    e0 = layer * e
    cap = idx.shape[0] // e
    tm = _tile(cap, 512)
    assert d_exp % MOE_STEPS == 0 and tm % 2 == 0
    tf = d_exp // MOE_STEPS
    tiles_per_expert = cap // tm
    kern = functools.partial(_moe_kernel, tm=tm, tiles_per_expert=tiles_per_expert,
                             n_tiles=e * tiles_per_expert, dn=_tile(d, 1024))
    grid_spec = pltpu.PrefetchScalarGridSpec(
        num_scalar_prefetch=1,
        grid=(e, tiles_per_expert, d_exp // tf),
        in_specs=[
            pl.BlockSpec(memory_space=pl.ANY),
            pl.BlockSpec((tm, 1), lambda ei, i, f, idx_: (ei * tiles_per_expert + i, 0)),
            pl.BlockSpec((1, d, tf), lambda ei, i, f, idx_: (e0 + ei, 0, f)),
            pl.BlockSpec((1, d, tf), lambda ei, i, f, idx_: (e0 + ei, 0, f)),
            pl.BlockSpec((1, tf, d), lambda ei, i, f, idx_: (e0 + ei, f, 0)),
            pl.BlockSpec(memory_space=pl.ANY),
        ],
        out_specs=pl.BlockSpec(memory_space=pl.ANY),
        scratch_shapes=[
            pltpu.VMEM((2, tm, d), F32),
            pltpu.VMEM((tm, d), BF16),
            pltpu.VMEM((tm, d), F32),
            pltpu.SemaphoreType.DMA((2,)),
            pltpu.SemaphoreType.DMA(()),
            pltpu.SemaphoreType.DMA(()),
        ],
    )
    return pl.pallas_call(
        kern,
        grid_spec=grid_spec,
        out_shape=jax.ShapeDtypeStruct((n, d), F32),
        input_output_aliases={6: 0},
        compiler_params=_params("arbitrary", "arbitrary", "arbitrary"),
        name="moe_experts",
    )(idx, hn, gate, wg, wu, wd, x2)


def _angles(pos, dim):
    inv_freq = ROPE_THETA ** (-jnp.arange(0, dim, 2, dtype=F32) / dim)
    return pos.astype(F32)[:, None] * inv_freq[None, :]


def _rope_tables_1d(s):
    ang = _angles(jnp.arange(s), HEAD_DIM)
    cos, sin = jnp.cos(ang), jnp.sin(ang)
    return jnp.concatenate([cos, cos], axis=1), jnp.concatenate([-sin, sin], axis=1)


def _rope_tables_axial(s):
    t = jnp.arange(s)
    half = HEAD_DIM // 2
    ar, ac = _angles(t // GRID_W, half), _angles(t % GRID_W, half)
    cos = jnp.concatenate([jnp.cos(ar), jnp.cos(ar), jnp.cos(ac), jnp.cos(ac)], axis=1)
    sin = jnp.concatenate([-jnp.sin(ar), jnp.sin(ar), -jnp.sin(ac), jnp.sin(ac)], axis=1)
    return cos, sin


def _trunk(x, norm_mix, norm_ffn, final_norm, a_w_qkv, a_w_o, a_sink, b_w_qkv, b_w_o, b_q_norm, b_k_norm,
           wr_t, w_gate, w_up, w_down):
    b, s, d = x.shape
    n = b * s
    depth = norm_mix.shape[0]
    qkv_dim = a_w_qkv.shape[2]
    def by_kind(cos, sin, q_scale):
        return (jnp.stack([cos * q_scale, cos, jnp.ones_like(cos)]),
                jnp.stack([sin * q_scale, sin, jnp.zeros_like(sin)]))

    cos_a, sin_a = by_kind(*_rope_tables_1d(s), 1.0)
    cos_b, sin_b = by_kind(*_rope_tables_axial(s), DENSE_Q_SCALE)
    ones = jnp.ones((HEAD_DIM,), F32)
    gain_a = jnp.stack([ones, ones, ones]).reshape(3, 1, HEAD_DIM)
    x2 = x.reshape(n, d)
    for i in range(depth):
        j = i // N_MIXERS
        if i % N_MIXERS == 0:
            qkv, vt = _qkv(x2, norm_mix[i], a_w_qkv[j], cos_a, sin_a, gain_a,
                           seq_len=s, qk_norm=False, axial=False)
            vt = vt.reshape(b, -1, vt.shape[1], vt.shape[2])
            sink_rows = jnp.repeat(a_sink[j].astype(F32), WINDOW).reshape(-1, 1, GROUP * WINDOW)
            o = _attn_window(qkv.reshape(b, s, qkv_dim), vt, sink_rows, d_model=d)
            x2 = _oproj(o.reshape(n, d), a_w_o[j], x2)
        else:
            gain_b = jnp.stack([b_q_norm[j].astype(F32), b_k_norm[j].astype(F32), ones]).reshape(3, 1, HEAD_DIM)
            qkv, vt = _qkv(x2, norm_mix[i], b_w_qkv[j], cos_b, sin_b, gain_b,
                           seq_len=s, qk_norm=True, axial=True)
            vt = vt.reshape(b, -1, vt.shape[1], vt.shape[2])
            o = _attn_dense(qkv.reshape(b, s, qkv_dim), vt, d_model=d)
            x2 = _oproj(o.reshape(n, d), b_w_o[j], x2)
        hn, aff_t = _norm_router(x2, norm_ffn[i], wr_t[i])
        idx, gate = _select(aff_t)
        x2 = _moe(x2, hn, idx, gate, w_gate, w_up, w_down, i)
    return _final_norm(x2, final_norm).reshape(b, s, d)


def kernel(x_prompt, x_sample, norm_mix, norm_ffn, final_norm, a_w_qkv, a_w_o, a_sink, b_w_qkv, b_w_o,
           b_q_norm, b_k_norm, w_router, w_gate, w_up, w_down):
    def experts(w):
        return w.astype(BF16).reshape((-1,) + w.shape[2:])

    weights = (a_w_qkv.astype(BF16), a_w_o.astype(BF16), a_sink, b_w_qkv.astype(BF16), b_w_o.astype(BF16),
               b_q_norm, b_k_norm, jnp.swapaxes(w_router, 1, 2).astype(BF16),
               experts(w_gate), experts(w_up), experts(w_down))
    y_prompt = _trunk(x_prompt, norm_mix, norm_ffn, final_norm, *weights)
    y_sample = _trunk(x_sample, norm_mix, norm_ffn, final_norm, *weights)
    return (y_prompt, y_sample)
```
